```python
import math
import jax, jax.numpy as jnp
from jax import lax
import numpy as np

D_MODEL = 1024
BATCH = 8
SEQ = 2048
DEPTH = 1
DEC_BATCH = 128
DEC_SEQ = 8
PAST_LEN = 16384
PAGE_SIZE = 128

H_R = 4
DK_R = 128
DV_R = 256
QK_R = H_R * DK_R
V_R = H_R * DV_R
H_M = 4
DH_M = 256
W_M = H_M * DH_M
CONV_W = 4
N_KEYS = 128
N_EXPERTS = N_KEYS * N_KEYS
P_HEADS = 8
P_DKEY = 256
P_HALF = P_DKEY // 2
P_TOPK = 16
P_BLOCK = 256
CHUNK = 128
ROPE_BASE = 10000.0
EPS = 1e-6
IN_SIZES = (QK_R, QK_R, V_R, V_R, W_M, W_M, W_M, H_M, H_M, D_MODEL, D_MODEL)
IN_WIDTH = 2 * QK_R + 2 * V_R + 3 * W_M + 2 * H_M + 2 * D_MODEL

kernel_name = 'hybrid_retention_mlstm_peer_step'


def _split_points():
    pts, acc = [], 0
    for s in IN_SIZES[:-1]:
        acc += s
        pts.append(acc)
    return pts


def rmsnorm(x, g):
    xf = x.astype(jnp.float32)
    y = xf * lax.rsqrt(jnp.mean(xf * xf, axis=-1, keepdims=True) + EPS) * g.astype(jnp.float32)
    return y.astype(x.dtype)


def head_norm(x, g):
    mu = jnp.mean(x, axis=-1, keepdims=True)
    xc = x - mu
    var = jnp.mean(xc * xc, axis=-1, keepdims=True)
    return xc * lax.rsqrt(var + EPS) * g.astype(jnp.float32)


def rope(x, pos):
    half = x.shape[-1] // 2
    freqs = jnp.exp(-math.log(ROPE_BASE) * jnp.arange(half, dtype=jnp.float32) / half)
    ang = pos[:, None] * freqs[None, :]
    cos = jnp.cos(ang)[None, :, None, :]
    sin = jnp.sin(ang)[None, :, None, :]
    x1, x2 = x[..., :half], x[..., half:]
    return jnp.concatenate([x1 * cos - x2 * sin, x1 * sin + x2 * cos], axis=-1)


def _chunk_len(T):
    return CHUNK if T % CHUNK == 0 else T


def _to_chunks(a, L):
    B, T = a.shape[0], a.shape[1]
    a = a.reshape((B, T // L, L) + a.shape[2:])
    perm = (1, 0, 3, 2) + tuple(range(4, a.ndim))
    return a.transpose(perm)


def _from_chunks(a):
    nc, B, H, L, d = a.shape
    return a.transpose(1, 0, 3, 2, 4).reshape(B, nc * L, H, d)


def retention_chunked(q, k, v, S0):
    T = q.shape[1]
    L = _chunk_len(T)
    log_gamma = jnp.log1p(-jnp.exp2(-5.0 - jnp.arange(H_R, dtype=jnp.float32)))
    idx = jnp.arange(L, dtype=jnp.float32)
    diff = idx[:, None] - idx[None, :]
    causal = diff >= 0
    dmat = jnp.where(causal[None], jnp.exp(log_gamma[:, None, None] * jnp.where(causal, diff, 0.0)[None]), 0.0)
    q_dec = jnp.exp(log_gamma[:, None] * (idx[None, :] + 1.0))
    k_dec = jnp.exp(log_gamma[:, None] * (L - 1.0 - idx[None, :]))
    c_dec = jnp.exp(log_gamma * L)

    def step(S, inp):
        qi, ki, vi = inp
        sc = jnp.einsum('bhtd,bhsd->bhts', qi, ki) * dmat[None]
        inner = jnp.einsum('bhts,bhsv->bhtv', sc, vi)
        cross = jnp.einsum('bhtd,bhdv->bhtv', qi, S) * q_dec[None, :, :, None]
        S_new = c_dec[None, :, None, None] * S + jnp.einsum('bhsd,bhsv->bhdv', ki * k_dec[None, :, :, None], vi)
        return S_new, inner + cross

    S, out = lax.scan(step, S0, (_to_chunks(q, L), _to_chunks(k, L), _to_chunks(v, L)))
    return _from_chunks(out), S


def mlstm_chunked(q, k, v, i_pre, logf, C0, n0, m0):
    T = q.shape[1]
    L = _chunk_len(T)
    ar = jnp.arange(L)
    causal = ar[:, None] >= ar[None, :]

    def step(carry, inp):
        C, n, m = carry
        qi, ki, vi, ii, fi = inp
        b = jnp.cumsum(fi, axis=-1)
        logD = jnp.where(causal, b[..., :, None] - b[..., None, :] + ii[..., None, :], -jnp.inf)
        m_t = jnp.maximum(b + m[..., None], jnp.max(logD, axis=-1))
        dw = jnp.exp(logD - m_t[..., None])
        w_prev = jnp.exp(b + m[..., None] - m_t)
        sc = jnp.einsum('bhtd,bhsd->bhts', qi, ki) * dw
        num = jnp.einsum('bhts,bhsv->bhtv', sc, vi) + w_prev[..., None] * jnp.einsum('bhtd,bhdv->bhtv', qi, C)
        den = jnp.sum(sc, axis=-1) + w_prev * jnp.einsum('bhtd,bhd->bht', qi, n)
        h = num / jnp.maximum(jnp.abs(den), jnp.exp(-m_t))[..., None]
        m_new = m_t[..., -1]
        wk = jnp.exp(b[..., -1:] - b + ii - m_new[..., None])
        dec = jnp.exp(b[..., -1] + m - m_new)
        C_new = dec[..., None, None] * C + jnp.einsum('bhsd,bhsv->bhdv', ki * wk[..., None], vi)
        n_new = dec[..., None] * n + jnp.einsum('bhs,bhsd->bhd', wk, ki)
        return (C_new, n_new, m_new), h

    (C, n, m), h = lax.scan(step, (C0, n0, m0),
                            (_to_chunks(q, L), _to_chunks(k, L), _to_chunks(v, L),
                             _to_chunks(i_pre, L), _to_chunks(logf, L)))
    return _from_chunks(h), C, n, m


def causal_conv(x, buf, w, b):
    T = x.shape[1]
    xp = jnp.concatenate([buf.astype(x.dtype), x], axis=1)
    out = b
    for j in range(CONV_W):
        out = out + xp[:, j:j + T] * w[j]
    return out, xp[:, xp.shape[1] - (CONV_W - 1):]


def peer(xn, w_pq, sub_keys1, sub_keys2, expert_u, expert_v):
    B, T, D = xn.shape
    dt = xn.dtype
    xt = xn.reshape(B * T, D)
    N = xt.shape[0]
    q = (xt @ w_pq).astype(jnp.float32).reshape(N, P_HEADS, 2, P_HALF)
    s1 = jnp.einsum('nhd,kd->nhk', q[:, :, 0], sub_keys1.astype(jnp.float32))
    s2 = jnp.einsum('nhd,kd->nhk', q[:, :, 1], sub_keys2.astype(jnp.float32))
    v1, i1 = lax.top_k(s1, P_TOPK)
    v2, i2 = lax.top_k(s2, P_TOPK)
    cand = (v1[..., :, None] + v2[..., None, :]).reshape(N, P_HEADS, P_TOPK * P_TOPK)
    cand_idx = (i1[..., :, None] * N_KEYS + i2[..., None, :]).reshape(N, P_HEADS, P_TOPK * P_TOPK)
    top_s, top_p = lax.top_k(cand, P_TOPK)
    idx = jnp.take_along_axis(cand_idx, top_p, axis=-1)
    g = jax.nn.softmax(top_s, axis=-1)
    n_blk = -(-N // P_BLOCK)
    pad = n_blk * P_BLOCK - N
    xb = jnp.pad(xt, ((0, pad), (0, 0))).reshape(n_blk, P_BLOCK, D)
    ib = jnp.pad(idx, ((0, pad), (0, 0), (0, 0))).reshape(n_blk, P_BLOCK, P_HEADS, P_TOPK)
    gb = jnp.pad(g, ((0, pad), (0, 0), (0, 0))).reshape(n_blk, P_BLOCK, P_HEADS, P_TOPK)

    def blk(args):
        xs, ids, gs = args
        u = expert_u[ids]
        a = jax.nn.gelu(jnp.einsum('pd,phkd->phk', xs, u).astype(jnp.float32), approximate=False)
        return jnp.einsum('phk,phkd->pd', (gs * a).astype(dt), expert_v[ids])

    out = lax.map(blk, (xb, ib, gb)).reshape(n_blk * P_BLOCK, D)[:N]
    return out.reshape(B, T, D)


def hybrid_layer(x, pos, s_ret, s_C, s_n, s_m, s_conv,
                 norm1_g, w_in, b_i, b_f, conv_w, conv_b, w_mq, w_mk, ret_norm_g, mlstm_norm_g,
                 w_br, w_bm, w_out, norm2_g, w_pq, sub_keys1, sub_keys2, expert_u, expert_v):
    B, T, _ = x.shape
    dt = x.dtype
    f32 = jnp.float32
    xn = rmsnorm(x, norm1_g)
    proj = xn @ w_in
    q_r, k_r, v_r, g_r, x_m, v_m, o_m, i_m, f_m, gate_r, gate_m = jnp.split(proj, _split_points(), axis=-1)
    q_r = rope(q_r.astype(f32).reshape(B, T, H_R, DK_R), pos) * (DK_R ** -0.5)
    k_r = rope(k_r.astype(f32).reshape(B, T, H_R, DK_R), pos)
    v_r = v_r.astype(f32).reshape(B, T, H_R, DV_R)
    y_r, ret_new = retention_chunked(q_r, k_r, v_r, s_ret.astype(f32))
    y_r = head_norm(y_r, ret_norm_g).reshape(B, T, V_R) * jax.nn.silu(g_r.astype(f32))
    br = y_r.astype(dt) @ w_br
    xc, conv_new = causal_conv(x_m, s_conv, conv_w, conv_b)
    xc = jax.nn.silu(xc.astype(f32)).reshape(B, T, H_M, DH_M)
    q_m = jnp.einsum('bthd,hde->bthe', xc, w_mq.astype(f32)) * (DH_M ** -0.5)
    k_m = jnp.einsum('bthd,hde->bthe', xc, w_mk.astype(f32))
    v_m = v_m.astype(f32).reshape(B, T, H_M, DH_M)
    i_pre = i_m.astype(f32) + b_i.astype(f32)
    logf = jax.nn.log_sigmoid(f_m.astype(f32) + b_f.astype(f32))
    h_m, C_new, n_new, m_new = mlstm_chunked(q_m, k_m, v_m, i_pre, logf,
                                             s_C.astype(f32), s_n.astype(f32), s_m.astype(f32))
    h_m = head_norm(h_m, mlstm_norm_g).reshape(B, T, W_M) * jax.nn.sigmoid(o_m.astype(f32))
    bm = h_m.astype(dt) @ w_bm
    merged = jax.nn.sigmoid(gate_r) * br + jax.nn.sigmoid(gate_m) * bm
    x = x + merged @ w_out
    x = x + peer(rmsnorm(x, norm2_g), w_pq, sub_keys1, sub_keys2, expert_u, expert_v)
    return x, ret_new, C_new, n_new, m_new, conv_new


def setup_inputs(seed: int = 0) -> dict:
    key = jax.random.key(seed)
    ks = jax.random.split(key, 32)
    f32 = jnp.float32

    def nrm(k, shape, scale):
        return jax.random.normal(k, shape, f32) * scale

    def gain(k, shape):
        return 1.0 + 0.01 * jax.random.normal(k, shape, f32)

    return {
        'x_prompt': nrm(ks[0], (BATCH, SEQ, D_MODEL), 1.0),
        'x_sample': nrm(ks[1], (DEC_BATCH, DEC_SEQ, D_MODEL), 1.0),
        'state_ret': nrm(ks[2], (DEPTH, DEC_BATCH, H_R, DK_R, DV_R), 2.0),
        'state_mlstm_C': nrm(ks[3], (DEPTH, DEC_BATCH, H_M, DH_M, DH_M), 0.5),
        'state_mlstm_n': nrm(ks[4], (DEPTH, DEC_BATCH, H_M, DH_M), 0.5),
        'state_mlstm_m': nrm(ks[5], (DEPTH, DEC_BATCH, H_M), 1.0),
        'state_conv': nrm(ks[6], (DEPTH, DEC_BATCH, CONV_W - 1, W_M), 1.0),
        'norm1_g': gain(ks[7], (DEPTH, D_MODEL)),
        'w_in': nrm(ks[8], (DEPTH, D_MODEL, IN_WIDTH), D_MODEL ** -0.5),
        'b_i': nrm(ks[9], (DEPTH, H_M), 0.1),
        'b_f': 3.0 + nrm(ks[10], (DEPTH, H_M), 0.5),
        'conv_w': nrm(ks[11], (DEPTH, CONV_W, W_M), 0.5),
        'conv_b': nrm(ks[12], (DEPTH, W_M), 0.02),
        'w_mq': nrm(ks[13], (DEPTH, H_M, DH_M, DH_M), DH_M ** -0.5),
        'w_mk': nrm(ks[14], (DEPTH, H_M, DH_M, DH_M), DH_M ** -0.5),
        'ret_norm_g': gain(ks[15], (DEPTH, H_R, DV_R)),
        'mlstm_norm_g': gain(ks[16], (DEPTH, H_M, DH_M)),
        'w_br': nrm(ks[17], (DEPTH, V_R, D_MODEL), V_R ** -0.5),
        'w_bm': nrm(ks[18], (DEPTH, W_M, D_MODEL), W_M ** -0.5),
        'w_out': nrm(ks[19], (DEPTH, D_MODEL, D_MODEL), D_MODEL ** -0.5),
        'norm2_g': gain(ks[20], (DEPTH, D_MODEL)),
        'w_pq': nrm(ks[21], (DEPTH, D_MODEL, P_HEADS * P_DKEY), D_MODEL ** -0.5),
        'sub_keys1': nrm(ks[22], (DEPTH, N_KEYS, P_HALF), P_HALF ** -0.5),
        'sub_keys2': nrm(ks[23], (DEPTH, N_KEYS, P_HALF), P_HALF ** -0.5),
        'expert_u': nrm(ks[24], (DEPTH, N_EXPERTS, D_MODEL), D_MODEL ** -0.5),
        'expert_v': nrm(ks[25], (DEPTH, N_EXPERTS, D_MODEL), P_HEADS ** -0.5),
        'norm_f_g': gain(ks[26], (D_MODEL,)),
    }


def reference(x_prompt, x_sample, state_ret, state_mlstm_C, state_mlstm_n, state_mlstm_m, state_conv,
              norm1_g, w_in, b_i, b_f, conv_w, conv_b, w_mq, w_mk, ret_norm_g, mlstm_norm_g,
              w_br, w_bm, w_out, norm2_g, w_pq, sub_keys1, sub_keys2, expert_u, expert_v, norm_f_g):
    f32 = jnp.float32
    Bp, Tp = x_prompt.shape[0], x_prompt.shape[1]
    Ts = x_sample.shape[1]
    pos_p = jnp.arange(Tp, dtype=f32)
    pos_s = PAST_LEN + jnp.arange(Ts, dtype=f32)
    hp, hs = x_prompt, x_sample
    ret_p, C_p, n_p, m_p, cv_p = [], [], [], [], []
    ret_s, C_s, n_s, m_s, cv_s = [], [], [], [], []
    for l in range(DEPTH):
        lw = (norm1_g[l], w_in[l], b_i[l], b_f[l], conv_w[l], conv_b[l], w_mq[l], w_mk[l],
              ret_norm_g[l], mlstm_norm_g[l], w_br[l], w_bm[l], w_out[l], norm2_g[l],
              w_pq[l], sub_keys1[l], sub_keys2[l], expert_u[l], expert_v[l])
        hp, r0, c0, n0, m0, v0 = hybrid_layer(
            hp, pos_p,
            jnp.zeros((Bp, H_R, DK_R, DV_R), f32), jnp.zeros((Bp, H_M, DH_M, DH_M), f32),
            jnp.zeros((Bp, H_M, DH_M), f32), jnp.zeros((Bp, H_M), f32),
            jnp.zeros((Bp, CONV_W - 1, W_M), x_prompt.dtype), *lw)
        hs, r1, c1, n1, m1, v1 = hybrid_layer(
            hs, pos_s, state_ret[l], state_mlstm_C[l], state_mlstm_n[l], state_mlstm_m[l], state_conv[l], *lw)
        ret_p.append(r0); C_p.append(c0); n_p.append(n0); m_p.append(m0); cv_p.append(v0)
        ret_s.append(r1); C_s.append(c1); n_s.append(n1); m_s.append(m1); cv_s.append(v1)
    y_prompt = rmsnorm(hp, norm_f_g)
    y_sample = rmsnorm(hs, norm_f_g)
    return (y_prompt, y_sample,
            jnp.stack(ret_p).astype(state_ret.dtype), jnp.stack(C_p).astype(state_mlstm_C.dtype),
            jnp.stack(n_p).astype(state_mlstm_n.dtype), jnp.stack(m_p).astype(state_mlstm_m.dtype),
            jnp.stack(cv_p).astype(state_conv.dtype),
            jnp.stack(ret_s).astype(state_ret.dtype), jnp.stack(C_s).astype(state_mlstm_C.dtype),
            jnp.stack(n_s).astype(state_mlstm_n.dtype), jnp.stack(m_s).astype(state_mlstm_m.dtype),
            jnp.stack(cv_s).astype(state_conv.dtype))
```

```python
import functools
import math

import jax
import jax.numpy as jnp
from jax import lax
from jax.experimental import pallas as pl
from jax.experimental.pallas import tpu as pltpu

F32 = jnp.float32
BF16 = jnp.bfloat16
HIGHEST = lax.Precision.HIGHEST

D_MODEL = 1024
H_R, DK_R, DV_R = 4, 128, 256
H_M, DH_M = 4, 256
CONV_W = 4
N_KEYS = 128
P_HEADS = 8
P_HALF = 128
P_TOPK = 16
CHUNK = 128
PAST_LEN = 16384
ROPE_BASE = 10000.0
EPS = 1e-6
LANES = 128
SUBLANES = 8
VMEM_LIMIT = 56 * 1024 * 1024
PROJ_MAIN = 8192
NEG_INF = float("-inf")


def _cparams(sem):
    return pltpu.CompilerParams(dimension_semantics=sem, vmem_limit_bytes=VMEM_LIMIT)


def _dot(a, b):
    return jnp.dot(a, b, preferred_element_type=F32)


def _dot_nt(a, b):
    return lax.dot_general(a, b, (((1,), (1,)), ((), ())), preferred_element_type=F32)


def _dot_tn(a, b):
    return lax.dot_general(a, b, (((0,), (0,)), ((), ())), preferred_element_type=F32)


def _sigmoid(x):
    return 1.0 / (1.0 + jnp.exp(-x))


def _log_sigmoid(x):
    return jnp.minimum(x, 0.0) - jnp.log(1.0 + jnp.exp(-jnp.abs(x)))


def _head_norm(x, g):
    mu = jnp.mean(x, axis=-1, keepdims=True)
    xc = x - mu
    var = jnp.mean(xc * xc, axis=-1, keepdims=True)
    return xc * lax.rsqrt(var + EPS) * g


def _proj_kernel(x_ref, g_ref, w_ref, wifc_ref, wifr_ref, o_ref, oc_ref, or_ref, xn_ref):
    @pl.when(pl.program_id(1) == 0)
    def _():
        x = x_ref[...]
        xn = x * lax.rsqrt(jnp.mean(x * x, axis=-1, keepdims=True) + EPS) * g_ref[...]
        xn_ref[...] = xn.astype(BF16)
        oc_ref[...] = jnp.dot(xn, wifc_ref[...], preferred_element_type=F32, precision=HIGHEST)
        or_ref[...] = lax.dot_general(wifr_ref[...], xn, (((1,), (1,)), ((), ())),
                                      preferred_element_type=F32, precision=HIGHEST)

    o_ref[...] = _dot(xn_ref[...], w_ref[...])


def _proj(x2d, g, w_main, wif_col, wif_row, tm, tn=1024):
    n = x2d.shape[0]
    return pl.pallas_call(
        _proj_kernel,
        grid=(n // tm, PROJ_MAIN // tn),
        in_specs=[
            pl.BlockSpec((tm, D_MODEL), lambda i, j: (i, 0)),
            pl.BlockSpec((1, D_MODEL), lambda i, j: (0, 0)),
            pl.BlockSpec((D_MODEL, tn), lambda i, j: (0, j)),
            pl.BlockSpec((D_MODEL, LANES), lambda i, j: (0, 0)),
            pl.BlockSpec((SUBLANES, D_MODEL), lambda i, j: (0, 0)),
        ],
        out_specs=[
            pl.BlockSpec((tm, tn), lambda i, j: (i, j)),
            pl.BlockSpec((tm, LANES), lambda i, j: (i, 0)),
            pl.BlockSpec((SUBLANES, tm), lambda i, j: (0, i)),
        ],
        out_shape=[
            jax.ShapeDtypeStruct((n, PROJ_MAIN), F32),
            jax.ShapeDtypeStruct((n, LANES), F32),
            jax.ShapeDtypeStruct((SUBLANES, n), F32),
        ],
        scratch_shapes=[pltpu.VMEM((tm, D_MODEL), BF16)],
        compiler_params=_cparams(("parallel", "arbitrary")),
    )(x2d, g, w_main, wif_col, wif_row)


def _ret_kernel(cdec_ref, q_ref, k_ref, v_ref, g_ref, cq_ref, sq_ref, ck_ref, sk_ref,
                dmat_ref, qdec_ref, kdec_ref, ng_ref, s0_ref, y_ref, s_ref):
    @pl.when(pl.program_id(1) == 0)
    def _():
        s_ref[...] = s0_ref[...]

    cq, sq, ck, sk = cq_ref[...], sq_ref[...], ck_ref[...], sk_ref[...]
    for h in range(H_R):
        q = q_ref[0, :, h * DK_R:(h + 1) * DK_R]
        k = k_ref[0, :, h * DK_R:(h + 1) * DK_R]
        v = v_ref[0, :, h * DV_R:(h + 1) * DV_R].astype(BF16)
        qr = (q * cq + pltpu.roll(q, DK_R // 2, 1) * sq).astype(BF16)
        kr = k * ck + pltpu.roll(k, DK_R // 2, 1) * sk
        s_old = s_ref[0, h]
        sc = _dot_nt(qr, kr.astype(BF16)) * dmat_ref[h]
        inner = _dot(sc.astype(BF16), v)
        cross = _dot(qr, s_old.astype(BF16)) * qdec_ref[h]
        s_ref[0, h] = cdec_ref[h] * s_old + _dot_tn((kr * kdec_ref[h]).astype(BF16), v)
        y = _head_norm(inner + cross, ng_ref[h])
        gate = g_ref[0, :, h * DV_R:(h + 1) * DV_R]
        y_ref[0, :, h * DV_R:(h + 1) * DV_R] = y * (gate * _sigmoid(gate))


def _retention(proj3, s0, pos, ret_norm_g):
    b, t, _ = proj3.shape
    l = CHUNK if t % CHUNK == 0 else t
    nc = t // l
    half = DK_R // 2
    freqs = jnp.exp(-math.log(ROPE_BASE) * jnp.arange(half, dtype=F32) / half)
    ang = pos[:, None] * freqs[None, :]
    cos, sin = jnp.cos(ang), jnp.sin(ang)
    cos2 = jnp.concatenate([cos, cos], axis=1)
    sin2 = jnp.concatenate([-sin, sin], axis=1)
    scale = DK_R ** -0.5
    log_gamma = jnp.log1p(-jnp.exp2(-5.0 - jnp.arange(H_R, dtype=F32)))
    idx = jnp.arange(l, dtype=F32)
    diff = idx[:, None] - idx[None, :]
    causal = diff >= 0
    dmat = jnp.where(causal[None], jnp.exp(log_gamma[:, None, None] * jnp.where(causal, diff, 0.0)[None]), 0.0)
    qdec = jnp.exp(log_gamma[:, None] * (idx[None, :] + 1.0))[:, :, None]
    kdec = jnp.exp(log_gamma[:, None] * (l - 1.0 - idx[None, :]))[:, :, None]
    cdec = jnp.exp(log_gamma * l)
    tab = pl.BlockSpec((l, DK_R), lambda i, c: (c, 0))
    return pl.pallas_call(
        _ret_kernel,
        grid=(b, nc),
        in_specs=[
            pl.BlockSpec(memory_space=pltpu.SMEM),
            pl.BlockSpec((1, l, H_R * DK_R), lambda i, c: (i, c, 0)),
            pl.BlockSpec((1, l, H_R * DK_R), lambda i, c: (i, c, 1)),
            pl.BlockSpec((1, l, H_R * DV_R), lambda i, c: (i, c, 1)),
            pl.BlockSpec((1, l, H_R * DV_R), lambda i, c: (i, c, 2)),
            tab, tab, tab, tab,
            pl.BlockSpec((H_R, l, l), lambda i, c: (0, 0, 0)),
            pl.BlockSpec((H_R, l, 1), lambda i, c: (0, 0, 0)),
            pl.BlockSpec((H_R, l, 1), lambda i, c: (0, 0, 0)),
            pl.BlockSpec((H_R, 1, DV_R), lambda i, c: (0, 0, 0)),
            pl.BlockSpec((1, H_R, DK_R, DV_R), lambda i, c: (i, 0, 0, 0)),
        ],
        out_specs=[
            pl.BlockSpec((1, l, H_R * DV_R), lambda i, c: (i, c, 0)),
            pl.BlockSpec((1, H_R, DK_R, DV_R), lambda i, c: (i, 0, 0, 0)),
        ],
        out_shape=[
            jax.ShapeDtypeStruct((b, t, H_R * DV_R), F32),
            jax.ShapeDtypeStruct((b, H_R, DK_R, DV_R), F32),
        ],
        compiler_params=_cparams(("parallel", "arbitrary")),
    )(cdec, proj3, proj3, proj3, proj3, cos2 * scale, sin2 * scale, cos2, sin2,
      dmat, qdec, kdec, ret_norm_g.reshape(H_R, 1, DV_R), s0)


def _mlstm_kernel(x_ref, v_ref, o_ref, ifc_ref, ifr_ref, bc_ref, br_ref, cw_ref, cb_ref,
                  wq_ref, wk_ref, ng_ref, conv0_ref, c0_ref, n0_ref, m0_ref,
                  h_ref, c_ref, n_ref, m_ref, xp_ref):
    l = x_ref.shape[1]

    @pl.when(pl.program_id(1) == 0)
    def _():
        xp_ref[0:SUBLANES, :] = conv0_ref[0]
        c_ref[...] = c0_ref[...]
        n_ref[...] = n0_ref[...]
        m_ref[...] = m0_ref[...]

    xp_ref[SUBLANES:SUBLANES + l, :] = x_ref[0]
    xc = cb_ref[...]
    for j in range(CONV_W):
        off = SUBLANES - (CONV_W - 1) + j
        xc = xc + xp_ref[off:off + l, :] * cw_ref[j:j + 1, :]
    xp_ref[0:SUBLANES, :] = xp_ref[l:l + SUBLANES, :]
    xs = xc * _sigmoid(xc)

    pre_c = ifc_ref[0] + bc_ref[...]
    pre_r = ifr_ref[0] + br_ref[...]
    ti = lax.broadcasted_iota(jnp.int32, (l, l), 0)
    si = lax.broadcasted_iota(jnp.int32, (l, l), 1)
    causal = si <= ti
    tri = causal.astype(F32)
    b_c = jnp.dot(tri, _log_sigmoid(pre_c), preferred_element_type=F32, precision=HIGHEST)
    b_r = lax.dot_general(_log_sigmoid(pre_r), tri, (((1,), (1,)), ((), ())),
                          preferred_element_type=F32, precision=HIGHEST)

    scale = DH_M ** -0.5
    for h in range(H_M):
        xh = xs[:, h * DH_M:(h + 1) * DH_M].astype(BF16)
        q = _dot(xh, wq_ref[h]) * scale
        k = _dot(xh, wk_ref[h])
        v = v_ref[0, :, h * DH_M:(h + 1) * DH_M].astype(BF16)
        qb = q.astype(BF16)
        c_old, n_old = c_ref[0, h], n_ref[0, h]
        m_old = m_ref[0, h][:, 0:1]
        bcol = b_c[:, H_M + h:H_M + h + 1]
        rcol = pre_c[:, h:h + 1] - bcol
        rrow = pre_r[h:h + 1, :] - b_r[H_M + h:H_M + h + 1, :]
        logd = jnp.where(causal, bcol + rrow, NEG_INF)
        m_t = jnp.maximum(bcol + m_old, jnp.max(logd, axis=-1, keepdims=True))
        dw = jnp.exp(logd - m_t)
        w_prev = jnp.exp(bcol + m_old - m_t)
        sc = _dot_nt(qb, k.astype(BF16)) * dw
        num = _dot(sc.astype(BF16), v) + w_prev * _dot(qb, c_old.astype(BF16))
        den = jnp.sum(sc, axis=-1, keepdims=True) + w_prev * jnp.sum(q * n_old, axis=-1, keepdims=True)
        hh = num / jnp.maximum(jnp.abs(den), jnp.exp(-m_t))
        m_new = m_t[l - 1:l, :]
        b_last = bcol[l - 1:l, :]
        wk = jnp.exp(b_last + rcol - m_new)
        dec = jnp.exp(b_last + m_old - m_new)
        kw = k * wk
        c_ref[0, h] = dec * c_old + _dot_tn(kw.astype(BF16), v)
        n_ref[0, h] = dec * n_old + jnp.sum(kw, axis=0, keepdims=True)
        m_ref[0, h] = jnp.broadcast_to(m_new, (1, LANES))
        og = o_ref[0, :, h * DH_M:(h + 1) * DH_M]
        h_ref[0, :, h * DH_M:(h + 1) * DH_M] = _head_norm(hh, ng_ref[h]) * _sigmoid(og)


def _mlstm(proj3, ifc3, ifr3, conv0, c0, n0, m0, b_i, b_f, conv_w, conv_b, w_mq, w_mk, norm_g):
    b, t, _ = proj3.shape
    l = CHUNK if t % CHUNK == 0 else t
    nc = t // l
    w = H_M * DH_M
    bias = jnp.concatenate([b_i, b_f]).astype(F32)
    bias_c = jnp.zeros((1, LANES), F32).at[0, :2 * H_M].set(bias)
    bias_r = bias.reshape(2 * H_M, 1)
    conv0p = jnp.concatenate([jnp.zeros((b, SUBLANES - (CONV_W - 1), w), F32), conv0], axis=1)
    const2 = lambda i, c: (0, 0)
    const3 = lambda i, c: (0, 0, 0)
    state4 = lambda i, c: (i, 0, 0, 0)
    return pl.pallas_call(
        _mlstm_kernel,
        grid=(b, nc),
        in_specs=[
            pl.BlockSpec((1, l, w), lambda i, c: (i, c, 3)),
            pl.BlockSpec((1, l, w), lambda i, c: (i, c, 4)),
            pl.BlockSpec((1, l, w), lambda i, c: (i, c, 5)),
            pl.BlockSpec((1, l, LANES), lambda i, c: (i, c, 0)),
            pl.BlockSpec((1, SUBLANES, l), lambda i, c: (i, 0, c)),
            pl.BlockSpec((1, LANES), const2),
            pl.BlockSpec((2 * H_M, 1), const2),
            pl.BlockSpec((CONV_W, w), const2),
            pl.BlockSpec((1, w), const2),
            pl.BlockSpec((H_M, DH_M, DH_M), const3),
            pl.BlockSpec((H_M, DH_M, DH_M), const3),
            pl.BlockSpec((H_M, 1, DH_M), const3),
            pl.BlockSpec((1, SUBLANES, w), lambda i, c: (i, 0, 0)),
            pl.BlockSpec((1, H_M, DH_M, DH_M), state4),
            pl.BlockSpec((1, H_M, 1, DH_M), state4),
            pl.BlockSpec((1, H_M, 1, LANES), state4),
        ],
        out_specs=[
            pl.BlockSpec((1, l, w), lambda i, c: (i, c, 0)),
            pl.BlockSpec((1, H_M, DH_M, DH_M), state4),
            pl.BlockSpec((1, H_M, 1, DH_M), state4),
            pl.BlockSpec((1, H_M, 1, LANES), state4),
        ],
        out_shape=[
            jax.ShapeDtypeStruct((b, t, w), F32),
            jax.ShapeDtypeStruct((b, H_M, DH_M, DH_M), F32),
            jax.ShapeDtypeStruct((b, H_M, 1, DH_M), F32),
            jax.ShapeDtypeStruct((b, H_M, 1, LANES), F32),
        ],
        scratch_shapes=[pltpu.VMEM((l + SUBLANES, w), F32)],
        compiler_params=_cparams(("parallel", "arbitrary")),
    )(proj3, proj3, proj3, ifc3, ifr3, bias_c, bias_r, conv_w, conv_b.reshape(1, w),
      w_mq.astype(BF16), w_mk.astype(BF16), norm_g.reshape(H_M, 1, DH_M), conv0p,
      c0, n0.reshape(b, H_M, 1, DH_M), jnp.broadcast_to(m0[:, :, None, None], (b, H_M, 1, LANES)))


def _merge_kernel(x_ref, yr_ref, hm_ref, gr_ref, gm_ref, wbr_ref, wbm_ref, wo_ref, g2_ref, x1_ref, xn_ref):
    br = _dot(yr_ref[...].astype(BF16), wbr_ref[...])
    bm = _dot(hm_ref[...].astype(BF16), wbm_ref[...])
    merged = _sigmoid(gr_ref[...]) * br + _sigmoid(gm_ref[...]) * bm
    x1 = x_ref[...] + _dot(merged.astype(BF16), wo_ref[...])
    x1_ref[...] = x1
    xn = x1 * lax.rsqrt(jnp.mean(x1 * x1, axis=-1, keepdims=True) + EPS) * g2_ref[...]
    xn_ref[...] = xn.astype(BF16)


def _merge(x2d, yr, hm, proj, w_br, w_bm, w_out, g2, tm):
    n = x2d.shape[0]
    row = lambda i: (i, 0)
    const = lambda i: (0, 0)
    wspec = pl.BlockSpec((D_MODEL, D_MODEL), const)
    return pl.pallas_call(
        _merge_kernel,
        grid=(n // tm,),
        in_specs=[
            pl.BlockSpec((tm, D_MODEL), row),
            pl.BlockSpec((tm, D_MODEL), row),
            pl.BlockSpec((tm, D_MODEL), row),
            pl.BlockSpec((tm, D_MODEL), lambda i: (i, 6)),
            pl.BlockSpec((tm, D_MODEL), lambda i: (i, 7)),
            wspec, wspec, wspec,
            pl.BlockSpec((1, D_MODEL), const),
        ],
        out_specs=[pl.BlockSpec((tm, D_MODEL), row), pl.BlockSpec((tm, D_MODEL), row)],
        out_shape=[jax.ShapeDtypeStruct((n, D_MODEL), F32), jax.ShapeDtypeStruct((n, D_MODEL), BF16)],
        compiler_params=_cparams(("parallel",)),
    )(x2d, yr, hm, proj, proj, w_br, w_bm, w_out, g2)


def _top16(s):
    rows = lax.broadcasted_iota(jnp.int32, (P_TOPK, s.shape[1]), 0)
    top = jnp.zeros((P_TOPK, s.shape[1]), F32)
    for a in range(P_TOPK):
        m = jnp.max(s, axis=0, keepdims=True)
        top = jnp.where(rows == a, m, top)
        s = jnp.where(s == m, NEG_INF, s)
    return top


def _route_kernel(xn_ref, wq_ref, k1_ref, k2_ref, e1_ref, thr_ref, s2_ref, e2_ref):
    tn = xn_ref.shape[0]
    qt = _dot_nt(wq_ref[...], xn_ref[...])
    for h in range(P_HEADS):
        base = h * 2 * P_HALF
        s1f = _dot(k1_ref[...], qt[base:base + P_HALF].astype(BF16))
        s2f = _dot(k2_ref[...], qt[base + P_HALF:base + 2 * P_HALF].astype(BF16))
        s2_ref[h] = s2f
        for c in range(tn // LANES):
            cols = slice(c * LANES, (c + 1) * LANES)
            s1, s2 = s1f[:, cols], s2f[:, cols]
            v1 = _top16(s1)
            v2 = _top16(s2)
            cand = [v1 + v2[0:1]]
            cand += [v1[0:SUBLANES] + v2[b:b + 1] for b in range(1, SUBLANES)]
            cand += [v2[SUBLANES:P_TOPK] + v1[0:1]]
            cand = jnp.concatenate(cand, axis=0)
            cmax = v1[0:1] + v2[0:1]
            z = jnp.zeros_like(cmax)
            tau = cmax
            for _ in range(P_TOPK):
                tau = jnp.max(cand, axis=0, keepdims=True)
                z = z + jnp.exp(tau - cmax)
                cand = jnp.where(cand == tau, NEG_INF, cand)
            thr = jnp.full_like(s1, float("inf"))
            for b in range(P_TOPK):
                vb = v2[b:b + 1]
                thr = jnp.where(s1 + vb >= tau, vb, thr)
            e1_ref[h, :, cols] = jnp.exp(s1 - v1[0:1])
            thr_ref[h, :, cols] = thr
            e2_ref[h, :, cols] = jnp.exp(s2 - v2[0:1]) / z


def _route(xn, wq_t, k1, k2, tn):
    n = xn.shape[0]
    const = lambda i: (0, 0)
    ospec = pl.BlockSpec((P_HEADS, N_KEYS, tn), lambda i: (0, 0, i))
    oshape = jax.ShapeDtypeStruct((P_HEADS, N_KEYS, n), F32)
    return pl.pallas_call(
        _route_kernel,
        grid=(n // tn,),
        in_specs=[
            pl.BlockSpec((tn, D_MODEL), lambda i: (i, 0)),
            pl.BlockSpec((P_HEADS * 2 * P_HALF, D_MODEL), const),
            pl.BlockSpec((N_KEYS, P_HALF), const),
            pl.BlockSpec((N_KEYS, P_HALF), const),
        ],
        out_specs=[ospec] * 4,
        out_shape=[oshape] * 4,
        compiler_params=_cparams(("parallel",)),
    )(xn, wq_t, k1, k2)


def _erf(x):
    return lax.erf(x)


def _peer_kernel(xn_ref, x1_ref, e1_ref, thr_ref, s2_ref, e2_ref, u_ref, vt_ref, gf_ref, y_ref, acc_ref, w_ref):
    j = pl.program_id(1)
    tn = xn_ref.shape[0]
    n_sub = u_ref.shape[0] // N_KEYS

    @pl.when(j == 0)
    def _():
        acc_ref[...] = jnp.zeros_like(acc_ref)

    assert n_sub == SUBLANES
    base = pl.multiple_of(j * n_sub, SUBLANES)
    ht = _dot_nt(u_ref[...], xn_ref[...])
    for c in range(tn // LANES):
        cols = slice(c * LANES, (c + 1) * LANES)
        thr8 = [thr_ref[h, pl.ds(base, SUBLANES), cols] for h in range(P_HEADS)]
        e18 = [e1_ref[h, pl.ds(base, SUBLANES), cols] for h in range(P_HEADS)]
        for r in range(n_sub):
            g = jnp.zeros((N_KEYS, LANES), F32)
            for h in range(P_HEADS):
                sel = jnp.where(s2_ref[h, :, cols] >= thr8[h][r:r + 1], e2_ref[h, :, cols], 0.0)
                g = g + e18[h][r:r + 1] * sel
            hh = ht[r * N_KEYS:(r + 1) * N_KEYS, cols]
            act = 0.5 * hh * (1.0 + _erf(hh * (2.0 ** -0.5)))
            w_ref[r * N_KEYS:(r + 1) * N_KEYS, cols] = (g * act).astype(BF16)
    acc_ref[...] += _dot(vt_ref[...], w_ref[...])

    @pl.when(j == pl.num_programs(1) - 1)
    def _():
        x = x1_ref[...] + acc_ref[...].T
        y_ref[...] = x * lax.rsqrt(jnp.mean(x * x, axis=-1, keepdims=True) + EPS) * gf_ref[...]


def _peer(xn, x1, e1, thr, s2, e2, u, vt, gf, tn, te=SUBLANES * N_KEYS):
    n = xn.shape[0]
    n_exp = u.shape[0]
    rspec = pl.BlockSpec((P_HEADS, N_KEYS, tn), lambda i, j: (0, 0, i))
    return pl.pallas_call(
        _peer_kernel,
        grid=(n // tn, n_exp // te),
        in_specs=[
            pl.BlockSpec((tn, D_MODEL), lambda i, j: (i, 0)),
            pl.BlockSpec((tn, D_MODEL), lambda i, j: (i, 0)),
            rspec, rspec, rspec, rspec,
            pl.BlockSpec((te, D_MODEL), lambda i, j: (j, 0)),
            pl.BlockSpec((D_MODEL, te), lambda i, j: (0, j)),
            pl.BlockSpec((1, D_MODEL), lambda i, j: (0, 0)),
        ],
        out_specs=pl.BlockSpec((tn, D_MODEL), lambda i, j: (i, 0)),
        out_shape=jax.ShapeDtypeStruct((n, D_MODEL), F32),
        scratch_shapes=[pltpu.VMEM((D_MODEL, tn), F32), pltpu.VMEM((te, tn), BF16)],
        compiler_params=_cparams(("parallel", "arbitrary")),
    )(xn, x1, e1, thr, s2, e2, u, vt, gf)


def _tile(n, pref):
    return pref if n % pref == 0 else n


def _layer(x, pos, s_ret, s_c, s_n, s_m, s_conv, wts):
    b, t, _ = x.shape
    n = b * t
    x2d = x.reshape(n, D_MODEL)
    proj, ifc, ifr = _proj(x2d, wts["norm1_g"], wts["w_main"], wts["wif_col"], wts["wif_row"], _tile(n, 1024))
    proj3 = proj.reshape(b, t, PROJ_MAIN)
    ifc3 = ifc.reshape(b, t, LANES)
    ifr3 = ifr.reshape(SUBLANES, b, t).transpose(1, 0, 2)
    y_r, ret_new = _retention(proj3, s_ret, pos, wts["ret_norm_g"])
    h_m, c_new, n_new, m_new = _mlstm(proj3, ifc3, ifr3, s_conv, s_c, s_n, s_m, wts["b_i"], wts["b_f"],
                                      wts["conv_w"], wts["conv_b"], wts["w_mq"], wts["w_mk"], wts["mlstm_norm_g"])
    conv_new = proj3[:, t - (CONV_W - 1):, 3 * D_MODEL:4 * D_MODEL]
    x1, xn2 = _merge(x2d, y_r.reshape(n, D_MODEL), h_m.reshape(n, D_MODEL), proj,
                     wts["w_br"], wts["w_bm"], wts["w_out"], wts["norm2_g"], _tile(n, 512))
    tn = _tile(n, 512)
    e1, thr, s2, e2 = _route(xn2, wts["w_pq_t"], wts["k1"], wts["k2"], tn)
    y = _peer(xn2, x1, e1, thr, s2, e2, wts["u"], wts["vt"], wts["norm_f_g"], tn)
    return (y.reshape(b, t, D_MODEL), ret_new, c_new, n_new.reshape(b, H_M, DH_M), m_new[:, :, 0, 0], conv_new)


def kernel(x_prompt, x_sample, state_ret, state_mlstm_C, state_mlstm_n, state_mlstm_m, state_conv, norm1_g, w_in, b_i, b_f, conv_w, conv_b, w_mq, w_mk, ret_norm_g, mlstm_norm_g, w_br, w_bm, w_out, norm2_g, w_pq, sub_keys1, sub_keys2, expert_u, expert_v, norm_f_g):
    assert w_in.shape[0] == 1, "single-layer step"
    bp, tp, _ = x_prompt.shape
    ts = x_sample.shape[1]
    qk, vr, wm = H_R * DK_R, H_R * DV_R, H_M * DH_M
    w = w_in[0]
    o_if = 2 * qk + 2 * vr + 3 * wm
    w_main = jnp.concatenate([w[:, :o_if], w[:, o_if + 2 * H_M:]], axis=1).astype(BF16)
    w_if = w[:, o_if:o_if + 2 * H_M]
    wts = dict(
        norm1_g=norm1_g[0].reshape(1, D_MODEL),
        w_main=w_main,
        wif_col=jnp.zeros((D_MODEL, LANES), F32).at[:, :2 * H_M].set(w_if),
        wif_row=w_if.T,
        b_i=b_i[0], b_f=b_f[0], conv_w=conv_w[0], conv_b=conv_b[0], w_mq=w_mq[0], w_mk=w_mk[0],
        ret_norm_g=ret_norm_g[0], mlstm_norm_g=mlstm_norm_g[0],
        w_br=w_br[0].astype(BF16), w_bm=w_bm[0].astype(BF16), w_out=w_out[0].astype(BF16),
        norm2_g=norm2_g[0].reshape(1, D_MODEL),
        w_pq_t=w_pq[0].T.astype(BF16),
        k1=sub_keys1[0].astype(BF16), k2=sub_keys2[0].astype(BF16),
        u=expert_u[0].astype(BF16), vt=expert_v[0].T.astype(BF16),
        norm_f_g=norm_f_g.reshape(1, D_MODEL),
    )
    pos_p = jnp.arange(tp, dtype=F32)
    pos_s = PAST_LEN + jnp.arange(ts, dtype=F32)
    zp = lambda *s: jnp.zeros((bp,) + s, F32)
    yp, r0, c0, n0, m0, v0 = _layer(x_prompt, pos_p, zp(H_R, DK_R, DV_R), zp(H_M, DH_M, DH_M), zp(H_M, DH_M),
                                    zp(H_M), zp(CONV_W - 1, wm), wts)
    ys, r1, c1, n1, m1, v1 = _layer(x_sample, pos_s, state_ret[0], state_mlstm_C[0], state_mlstm_n[0],
                                    state_mlstm_m[0], state_conv[0], wts)
    return (yp, ys, r0[None], c0[None], n0[None], m0[None], v0[None],
            r1[None], c1[None], n1[None], m1[None], v1[None])
```

```python
import functools
import math

import jax
import jax.numpy as jnp
from jax import lax
from jax.experimental import pallas as pl
from jax.experimental.pallas import tpu as pltpu

F32 = jnp.float32
BF16 = jnp.bfloat16
HIGHEST = lax.Precision.HIGHEST

D_MODEL = 1024
H_R, DK_R, DV_R = 4, 128, 256
H_M, DH_M = 4, 256
CONV_W = 4
N_KEYS = 128
P_HEADS = 8
P_HALF = 128
P_TOPK = 16
CHUNK = 128
PAST_LEN = 16384
ROPE_BASE = 10000.0
EPS = 1e-6
LANES = 128
SUBLANES = 8
PEER_CHAINS = 2
VMEM_LIMIT = 56 * 1024 * 1024
PROJ_MAIN = 8192
NEG_INF = float("-inf")


def _cparams(sem):
    return pltpu.CompilerParams(dimension_semantics=sem, vmem_limit_bytes=VMEM_LIMIT)


def _dot(a, b):
    return jnp.dot(a, b, preferred_element_type=F32)


def _dot_nt(a, b):
    return lax.dot_general(a, b, (((1,), (1,)), ((), ())), preferred_element_type=F32)


def _dot_tn(a, b):
    return lax.dot_general(a, b, (((0,), (0,)), ((), ())), preferred_element_type=F32)


def _sigmoid(x):
    return 1.0 / (1.0 + jnp.exp(-x))


def _log_sigmoid(x):
    return jnp.minimum(x, 0.0) - jnp.log(1.0 + jnp.exp(-jnp.abs(x)))


def _head_norm(x, g):
    mu = jnp.mean(x, axis=-1, keepdims=True)
    xc = x - mu
    var = jnp.mean(xc * xc, axis=-1, keepdims=True)
    return xc * lax.rsqrt(var + EPS) * g


def _proj_kernel(x_ref, g_ref, w_ref, wifc_ref, wifr_ref, o_ref, oc_ref, or_ref, xn_ref):
    @pl.when(pl.program_id(1) == 0)
    def _():
        x = x_ref[...]
        xn = x * lax.rsqrt(jnp.mean(x * x, axis=-1, keepdims=True) + EPS) * g_ref[...]
        xn_ref[...] = xn.astype(BF16)
        oc_ref[...] = jnp.dot(xn, wifc_ref[...], preferred_element_type=F32, precision=HIGHEST)
        or_ref[...] = lax.dot_general(wifr_ref[...], xn, (((1,), (1,)), ((), ())),
                                      preferred_element_type=F32, precision=HIGHEST)

    o_ref[...] = _dot(xn_ref[...], w_ref[...])


def _proj(x2d, g, w_main, wif_col, wif_row, tm, tn=1024):
    n = x2d.shape[0]
    return pl.pallas_call(
        _proj_kernel,
        grid=(n // tm, PROJ_MAIN // tn),
        in_specs=[
            pl.BlockSpec((tm, D_MODEL), lambda i, j: (i, 0)),
            pl.BlockSpec((1, D_MODEL), lambda i, j: (0, 0)),
            pl.BlockSpec((D_MODEL, tn), lambda i, j: (0, j)),
            pl.BlockSpec((D_MODEL, LANES), lambda i, j: (0, 0)),
            pl.BlockSpec((SUBLANES, D_MODEL), lambda i, j: (0, 0)),
        ],
        out_specs=[
            pl.BlockSpec((tm, tn), lambda i, j: (i, j)),
            pl.BlockSpec((tm, LANES), lambda i, j: (i, 0)),
            pl.BlockSpec((SUBLANES, tm), lambda i, j: (0, i)),
        ],
        out_shape=[
            jax.ShapeDtypeStruct((n, PROJ_MAIN), F32),
            jax.ShapeDtypeStruct((n, LANES), F32),
            jax.ShapeDtypeStruct((SUBLANES, n), F32),
        ],
        scratch_shapes=[pltpu.VMEM((tm, D_MODEL), BF16)],
        compiler_params=_cparams(("parallel", "arbitrary")),
    )(x2d, g, w_main, wif_col, wif_row)


def _batch_block(b, nc):
    if nc > 1:
        return 1
    for nb in (4, 2):
        if b % nb == 0:
            return nb
    return 1


def _ret_kernel(cdec_ref, q_ref, k_ref, v_ref, g_ref, cq_ref, sq_ref, ck_ref, sk_ref,
                dmat_ref, qdec_ref, kdec_ref, ng_ref, s0_ref, y_ref, s_ref):
    @pl.when(pl.program_id(1) == 0)
    def _():
        s_ref[...] = s0_ref[...]

    cq, sq, ck, sk = cq_ref[...], sq_ref[...], ck_ref[...], sk_ref[...]
    for bb in range(q_ref.shape[0]):
        for h in range(H_R):
            q = q_ref[bb, :, h * DK_R:(h + 1) * DK_R]
            k = k_ref[bb, :, h * DK_R:(h + 1) * DK_R]
            v = v_ref[bb, :, h * DV_R:(h + 1) * DV_R].astype(BF16)
            qr = (q * cq + pltpu.roll(q, DK_R // 2, 1) * sq).astype(BF16)
            kr = k * ck + pltpu.roll(k, DK_R // 2, 1) * sk
            s_old = s_ref[bb, h]
            sc = _dot_nt(qr, kr.astype(BF16)) * dmat_ref[h]
            inner = _dot(sc.astype(BF16), v)
            cross = _dot(qr, s_old.astype(BF16)) * qdec_ref[h]
            s_ref[bb, h] = cdec_ref[h] * s_old + _dot_tn((kr * kdec_ref[h]).astype(BF16), v)
            y = _head_norm(inner + cross, ng_ref[h])
            gate = g_ref[bb, :, h * DV_R:(h + 1) * DV_R]
            y_ref[bb, :, h * DV_R:(h + 1) * DV_R] = y * (gate * _sigmoid(gate))


def _retention(proj3, s0, pos, ret_norm_g):
    b, t, _ = proj3.shape
    l = CHUNK if t % CHUNK == 0 else t
    nc = t // l
    half = DK_R // 2
    freqs = jnp.exp(-math.log(ROPE_BASE) * jnp.arange(half, dtype=F32) / half)
    ang = pos[:, None] * freqs[None, :]
    cos, sin = jnp.cos(ang), jnp.sin(ang)
    cos2 = jnp.concatenate([cos, cos], axis=1)
    sin2 = jnp.concatenate([-sin, sin], axis=1)
    scale = DK_R ** -0.5
    log_gamma = jnp.log1p(-jnp.exp2(-5.0 - jnp.arange(H_R, dtype=F32)))
    idx = jnp.arange(l, dtype=F32)
    diff = idx[:, None] - idx[None, :]
    causal = diff >= 0
    dmat = jnp.where(causal[None], jnp.exp(log_gamma[:, None, None] * jnp.where(causal, diff, 0.0)[None]), 0.0)
    qdec = jnp.exp(log_gamma[:, None] * (idx[None, :] + 1.0))[:, :, None]
    kdec = jnp.exp(log_gamma[:, None] * (l - 1.0 - idx[None, :]))[:, :, None]
    cdec = jnp.exp(log_gamma * l)
    tab = pl.BlockSpec((l, DK_R), lambda i, c: (c, 0))
    nb = _batch_block(b, nc)
    return pl.pallas_call(
        _ret_kernel,
        grid=(b // nb, nc),
        in_specs=[
            pl.BlockSpec(memory_space=pltpu.SMEM),
            pl.BlockSpec((nb, l, H_R * DK_R), lambda i, c: (i, c, 0)),
            pl.BlockSpec((nb, l, H_R * DK_R), lambda i, c: (i, c, 1)),
            pl.BlockSpec((nb, l, H_R * DV_R), lambda i, c: (i, c, 1)),
            pl.BlockSpec((nb, l, H_R * DV_R), lambda i, c: (i, c, 2)),
            tab, tab, tab, tab,
            pl.BlockSpec((H_R, l, l), lambda i, c: (0, 0, 0)),
            pl.BlockSpec((H_R, l, 1), lambda i, c: (0, 0, 0)),
            pl.BlockSpec((H_R, l, 1), lambda i, c: (0, 0, 0)),
            pl.BlockSpec((H_R, 1, DV_R), lambda i, c: (0, 0, 0)),
            pl.BlockSpec((nb, H_R, DK_R, DV_R), lambda i, c: (i, 0, 0, 0)),
        ],
        out_specs=[
            pl.BlockSpec((nb, l, H_R * DV_R), lambda i, c: (i, c, 0)),
            pl.BlockSpec((nb, H_R, DK_R, DV_R), lambda i, c: (i, 0, 0, 0)),
        ],
        out_shape=[
            jax.ShapeDtypeStruct((b, t, H_R * DV_R), F32),
            jax.ShapeDtypeStruct((b, H_R, DK_R, DV_R), F32),
        ],
        compiler_params=_cparams(("parallel", "arbitrary")),
    )(cdec, proj3, proj3, proj3, proj3, cos2 * scale, sin2 * scale, cos2, sin2,
      dmat, qdec, kdec, ret_norm_g.reshape(H_R, 1, DV_R), s0)


def _mlstm_kernel(x_ref, v_ref, o_ref, ifc_ref, ifr_ref, bc_ref, br_ref, cw_ref, cb_ref,
                  wq_ref, wk_ref, ng_ref, conv0_ref, c0_ref, n0_ref, m0_ref,
                  h_ref, c_ref, n_ref, m_ref, xp_ref):
    l = x_ref.shape[1]

    @pl.when(pl.program_id(1) == 0)
    def _():
        xp_ref[:, 0:SUBLANES, :] = conv0_ref[...]
        c_ref[...] = c0_ref[...]
        n_ref[...] = n0_ref[...]
        m_ref[...] = m0_ref[...]

    ti = lax.broadcasted_iota(jnp.int32, (l, l), 0)
    si = lax.broadcasted_iota(jnp.int32, (l, l), 1)
    causal = si <= ti
    tri = causal.astype(F32)
    for bb in range(x_ref.shape[0]):
        _mlstm_row(bb, l, causal, tri, x_ref, v_ref, o_ref, ifc_ref, ifr_ref, bc_ref, br_ref, cw_ref, cb_ref,
                   wq_ref, wk_ref, ng_ref, h_ref, c_ref, n_ref, m_ref, xp_ref)


def _mlstm_row(bb, l, causal, tri, x_ref, v_ref, o_ref, ifc_ref, ifr_ref, bc_ref, br_ref, cw_ref, cb_ref,
               wq_ref, wk_ref, ng_ref, h_ref, c_ref, n_ref, m_ref, xp_ref):
    xp_ref[bb, SUBLANES:SUBLANES + l, :] = x_ref[bb]
    xc = cb_ref[...]
    for j in range(CONV_W):
        off = SUBLANES - (CONV_W - 1) + j
        xc = xc + xp_ref[bb, off:off + l, :] * cw_ref[j:j + 1, :]
    xp_ref[bb, 0:SUBLANES, :] = xp_ref[bb, l:l + SUBLANES, :]
    xs = xc * _sigmoid(xc)

    pre_c = ifc_ref[bb] + bc_ref[...]
    pre_r = ifr_ref[bb] + br_ref[...]
    b_c = jnp.dot(tri, _log_sigmoid(pre_c), preferred_element_type=F32, precision=HIGHEST)
    b_r = lax.dot_general(_log_sigmoid(pre_r), tri, (((1,), (1,)), ((), ())),
                          preferred_element_type=F32, precision=HIGHEST)

    scale = DH_M ** -0.5
    for h in range(H_M):
        xh = xs[:, h * DH_M:(h + 1) * DH_M].astype(BF16)
        q = _dot(xh, wq_ref[h]) * scale
        k = _dot(xh, wk_ref[h])
        v = v_ref[bb, :, h * DH_M:(h + 1) * DH_M].astype(BF16)
        qb = q.astype(BF16)
        c_old, n_old = c_ref[bb, h], n_ref[bb, h]
        m_old = m_ref[bb, h][:, 0:1]
        bcol = b_c[:, H_M + h:H_M + h + 1]
        rcol = pre_c[:, h:h + 1] - bcol
        rrow = pre_r[h:h + 1, :] - b_r[H_M + h:H_M + h + 1, :]
        logd = jnp.where(causal, bcol + rrow, NEG_INF)
        m_t = jnp.maximum(bcol + m_old, jnp.max(logd, axis=-1, keepdims=True))
        dw = jnp.exp(logd - m_t)
        w_prev = jnp.exp(bcol + m_old - m_t)
        sc = _dot_nt(qb, k.astype(BF16)) * dw
        num = _dot(sc.astype(BF16), v) + w_prev * _dot(qb, c_old.astype(BF16))
        den = jnp.sum(sc, axis=-1, keepdims=True) + w_prev * jnp.sum(q * n_old, axis=-1, keepdims=True)
        hh = num / jnp.maximum(jnp.abs(den), jnp.exp(-m_t))
        m_new = m_t[l - 1:l, :]
        b_last = bcol[l - 1:l, :]
        wk = jnp.exp(b_last + rcol - m_new)
        dec = jnp.exp(b_last + m_old - m_new)
        kw = k * wk
        c_ref[bb, h] = dec * c_old + _dot_tn(kw.astype(BF16), v)
        n_ref[bb, h] = dec * n_old + jnp.sum(kw, axis=0, keepdims=True)
        m_ref[bb, h] = jnp.broadcast_to(m_new, (1, LANES))
        og = o_ref[bb, :, h * DH_M:(h + 1) * DH_M]
        h_ref[bb, :, h * DH_M:(h + 1) * DH_M] = _head_norm(hh, ng_ref[h]) * _sigmoid(og)


def _mlstm(proj3, ifc3, ifr3, conv0, c0, n0, m0, b_i, b_f, conv_w, conv_b, w_mq, w_mk, norm_g):
    b, t, _ = proj3.shape
    l = CHUNK if t % CHUNK == 0 else t
    nc = t // l
    w = H_M * DH_M
    bias = jnp.concatenate([b_i, b_f]).astype(F32)
    bias_c = jnp.zeros((1, LANES), F32).at[0, :2 * H_M].set(bias)
    bias_r = bias.reshape(2 * H_M, 1)
    conv0p = jnp.concatenate([jnp.zeros((b, SUBLANES - (CONV_W - 1), w), F32), conv0], axis=1)
    const2 = lambda i, c: (0, 0)
    const3 = lambda i, c: (0, 0, 0)
    state4 = lambda i, c: (i, 0, 0, 0)
    nb = _batch_block(b, nc)
    return pl.pallas_call(
        _mlstm_kernel,
        grid=(b // nb, nc),
        in_specs=[
            pl.BlockSpec((nb, l, w), lambda i, c: (i, c, 3)),
            pl.BlockSpec((nb, l, w), lambda i, c: (i, c, 4)),
            pl.BlockSpec((nb, l, w), lambda i, c: (i, c, 5)),
            pl.BlockSpec((nb, l, LANES), lambda i, c: (i, c, 0)),
            pl.BlockSpec((nb, SUBLANES, l), lambda i, c: (i, 0, c)),
            pl.BlockSpec((1, LANES), const2),
            pl.BlockSpec((2 * H_M, 1), const2),
            pl.BlockSpec((CONV_W, w), const2),
            pl.BlockSpec((1, w), const2),
            pl.BlockSpec((H_M, DH_M, DH_M), const3),
            pl.BlockSpec((H_M, DH_M, DH_M), const3),
            pl.BlockSpec((H_M, 1, DH_M), const3),
            pl.BlockSpec((nb, SUBLANES, w), lambda i, c: (i, 0, 0)),
            pl.BlockSpec((nb, H_M, DH_M, DH_M), state4),
            pl.BlockSpec((nb, H_M, 1, DH_M), state4),
            pl.BlockSpec((nb, H_M, 1, LANES), state4),
        ],
        out_specs=[
            pl.BlockSpec((nb, l, w), lambda i, c: (i, c, 0)),
            pl.BlockSpec((nb, H_M, DH_M, DH_M), state4),
            pl.BlockSpec((nb, H_M, 1, DH_M), state4),
            pl.BlockSpec((nb, H_M, 1, LANES), state4),
        ],
        out_shape=[
            jax.ShapeDtypeStruct((b, t, w), F32),
            jax.ShapeDtypeStruct((b, H_M, DH_M, DH_M), F32),
            jax.ShapeDtypeStruct((b, H_M, 1, DH_M), F32),
            jax.ShapeDtypeStruct((b, H_M, 1, LANES), F32),
        ],
        scratch_shapes=[pltpu.VMEM((nb, l + SUBLANES, w), F32)],
        compiler_params=_cparams(("parallel", "arbitrary")),
    )(proj3, proj3, proj3, ifc3, ifr3, bias_c, bias_r, conv_w, conv_b.reshape(1, w),
      w_mq.astype(BF16), w_mk.astype(BF16), norm_g.reshape(H_M, 1, DH_M), conv0p,
      c0, n0.reshape(b, H_M, 1, DH_M), jnp.broadcast_to(m0[:, :, None, None], (b, H_M, 1, LANES)))


def _merge_kernel(x_ref, yr_ref, hm_ref, gr_ref, gm_ref, wbr_ref, wbm_ref, wo_ref, g2_ref, x1_ref, xn_ref):
    br = _dot(yr_ref[...].astype(BF16), wbr_ref[...])
    bm = _dot(hm_ref[...].astype(BF16), wbm_ref[...])
    merged = _sigmoid(gr_ref[...]) * br + _sigmoid(gm_ref[...]) * bm
    x1 = x_ref[...] + _dot(merged.astype(BF16), wo_ref[...])
    x1_ref[...] = x1
    xn = x1 * lax.rsqrt(jnp.mean(x1 * x1, axis=-1, keepdims=True) + EPS) * g2_ref[...]
    xn_ref[...] = pltpu.bitcast(xn.astype(BF16), jnp.int32)


def _merge(x2d, yr, hm, proj, w_br, w_bm, w_out, g2, tm):
    n = x2d.shape[0]
    row = lambda i: (i, 0)
    const = lambda i: (0, 0)
    wspec = pl.BlockSpec((D_MODEL, D_MODEL), const)
    return pl.pallas_call(
        _merge_kernel,
        grid=(n // tm,),
        in_specs=[
            pl.BlockSpec((tm, D_MODEL), row),
            pl.BlockSpec((tm, D_MODEL), row),
            pl.BlockSpec((tm, D_MODEL), row),
            pl.BlockSpec((tm, D_MODEL), lambda i: (i, 6)),
            pl.BlockSpec((tm, D_MODEL), lambda i: (i, 7)),
            wspec, wspec, wspec,
            pl.BlockSpec((1, D_MODEL), const),
        ],
        out_specs=[pl.BlockSpec((tm, D_MODEL), row), pl.BlockSpec((tm // 2, D_MODEL), row)],
        out_shape=[jax.ShapeDtypeStruct((n, D_MODEL), F32), jax.ShapeDtypeStruct((n // 2, D_MODEL), jnp.int32)],
        compiler_params=_cparams(("parallel",)),
    )(x2d, yr, hm, proj, proj, w_br, w_bm, w_out, g2)


def _top16(s):
    rows = lax.broadcasted_iota(jnp.int32, (P_TOPK, s.shape[1]), 0)
    top = jnp.zeros((P_TOPK, s.shape[1]), F32)
    rank = jnp.full(s.shape, float(P_TOPK), F32)
    for a in range(P_TOPK):
        m = jnp.max(s, axis=0, keepdims=True)
        hit = s == m
        top = jnp.where(rows == a, m, top)
        rank = jnp.where(hit, float(a), rank)
        s = jnp.where(hit, NEG_INF, s)
    return top, rank


def _bf16_pair(x):
    b = pltpu.bitcast(x.astype(BF16).astype(F32), jnp.uint32)
    return pltpu.bitcast(b | (b >> 16), jnp.int32)


def _route_kernel(xn_ref, wq_ref, k1_ref, k2_ref, e1_ref, nb_ref, rk_ref, e2_ref):
    tn = 2 * xn_ref.shape[0]
    qt = _dot_nt(wq_ref[...], pltpu.bitcast(xn_ref[...], BF16))
    for h in range(P_HEADS):
        base = h * 2 * P_HALF
        s1f = _dot(k1_ref[...], qt[base:base + P_HALF].astype(BF16))
        s2f = _dot(k2_ref[...], qt[base + P_HALF:base + 2 * P_HALF].astype(BF16))
        for c in range(tn // LANES):
            cols = slice(c * LANES, (c + 1) * LANES)
            s1, s2 = s1f[:, cols], s2f[:, cols]
            v1, _ = _top16(s1)
            v2, rank2 = _top16(s2)
            cand = [v1 + v2[0:1]]
            cand += [v1[0:SUBLANES] + v2[b:b + 1] for b in range(1, SUBLANES)]
            cand += [v2[SUBLANES:P_TOPK] + v1[0:1]]
            cand = jnp.concatenate(cand, axis=0)
            cmax = v1[0:1] + v2[0:1]
            z = jnp.zeros_like(cmax)
            tau = cmax
            for _ in range(P_TOPK):
                tau = jnp.max(cand, axis=0, keepdims=True)
                z = z + jnp.exp(tau - cmax)
                cand = jnp.where(cand == tau, NEG_INF, cand)
            nb = jnp.zeros_like(s1)
            for b in range(P_TOPK):
                nb = jnp.where(s1 + v2[b:b + 1] >= tau, float(b + 1), nb)
            e1_ref[h, :, cols] = _bf16_pair(jnp.exp(s1 - v1[0:1]))
            nb_ref[h, :, cols] = _bf16_pair(nb)
            rk_ref[h, :, cols] = pltpu.bitcast(rank2.astype(BF16), jnp.int32)
            e2_ref[h, :, cols] = pltpu.bitcast((jnp.exp(s2 - v2[0:1]) / z).astype(BF16), jnp.int32)


def _route(xn, wq_t, k1, k2, tn):
    n = 2 * xn.shape[0]
    const = lambda i: (0, 0)
    ospec = pl.BlockSpec((P_HEADS, N_KEYS, tn), lambda i: (0, 0, i))
    pspec = pl.BlockSpec((P_HEADS, N_KEYS // 2, tn), lambda i: (0, 0, i))
    oshape = jax.ShapeDtypeStruct((P_HEADS, N_KEYS, n), jnp.int32)
    pshape = jax.ShapeDtypeStruct((P_HEADS, N_KEYS // 2, n), jnp.int32)
    return pl.pallas_call(
        _route_kernel,
        grid=(n // tn,),
        in_specs=[
            pl.BlockSpec((tn // 2, D_MODEL), lambda i: (i, 0)),
            pl.BlockSpec((P_HEADS * 2 * P_HALF, D_MODEL), const),
            pl.BlockSpec((N_KEYS, P_HALF), const),
            pl.BlockSpec((N_KEYS, P_HALF), const),
        ],
        out_specs=[ospec, ospec, pspec, pspec],
        out_shape=[oshape, oshape, pshape, pshape],
        compiler_params=_cparams(("parallel",)),
    )(xn, wq_t, k1, k2)


def _erf(x):
    return lax.erf(x)


def _pack_kernel(x_ref, o_ref, *, transpose):
    x = x_ref[...]
    o_ref[...] = pltpu.bitcast((x.T if transpose else x).astype(BF16), jnp.int32)


def _pack_bf16(x, transpose, blk=1024):
    r, c = x.shape
    if transpose:
        out_spec, out_shape = pl.BlockSpec((c // 2, blk), lambda i: (0, i)), (c // 2, r)
    else:
        out_spec, out_shape = pl.BlockSpec((blk // 2, c), lambda i: (i, 0)), (r // 2, c)
    return pl.pallas_call(
        functools.partial(_pack_kernel, transpose=transpose),
        grid=(r // blk,),
        in_specs=[pl.BlockSpec((blk, c), lambda i: (i, 0))],
        out_specs=out_spec,
        out_shape=jax.ShapeDtypeStruct(out_shape, jnp.int32),
        compiler_params=_cparams(("parallel",)),
    )(x)


def _peer_kernel(xn_ref, x1_ref, e1_ref, nb_ref, rk_ref, e2_ref, u_ref, vt_ref, gf_ref, y_ref, acc_ref, w_ref):
    j = pl.program_id(1)
    tn = x1_ref.shape[0]
    n_sub = 2 * u_ref.shape[0] // N_KEYS
    pk = 2 * SUBLANES
    grp = (N_KEYS // pk, pk, LANES)
    r_share = 2

    @pl.when(j == 0)
    def _():
        acc_ref[...] = jnp.zeros_like(acc_ref)

    assert n_sub == SUBLANES
    base = pl.multiple_of(j * n_sub, SUBLANES)
    xn = pltpu.bitcast(xn_ref[...], BF16)
    te = n_sub * N_KEYS
    ce = te // PEER_CHAINS
    out = None
    for e0 in range(0, te, ce):
        u = pltpu.bitcast(u_ref[e0 // 2:(e0 + ce) // 2, :], BF16)
        ht = _dot_nt(u, xn)
        for c0 in range(0, tn, LANES):
            cols = slice(c0, c0 + LANES)
            nb8 = [nb_ref[h, pl.ds(base, SUBLANES), cols] for h in range(P_HEADS)]
            e18 = [e1_ref[h, pl.ds(base, SUBLANES), cols] for h in range(P_HEADS)]
            for r0 in range(e0 // N_KEYS, (e0 + ce) // N_KEYS, r_share):
                g = [jnp.zeros(grp, BF16) for _ in range(r_share)]
                for h in range(P_HEADS):
                    rk = pltpu.bitcast(rk_ref[h, :, cols], BF16).reshape(grp)
                    e2 = pltpu.bitcast(e2_ref[h, :, cols], BF16).reshape(grp)
                    for k in range(r_share):
                        r = r0 + k
                        nbb = pltpu.bitcast(jnp.broadcast_to(nb8[h][r:r + 1], (SUBLANES, LANES)), BF16)
                        e1b = pltpu.bitcast(jnp.broadcast_to(e18[h][r:r + 1], (SUBLANES, LANES)), BF16)
                        g[k] = g[k] + e1b[None] * jnp.where(rk < nbb[None], e2, jnp.zeros_like(e2))
                for k in range(r_share):
                    rows = slice((r0 + k) * N_KEYS, (r0 + k + 1) * N_KEYS)
                    hh = ht[rows.start - e0:rows.stop - e0, cols]
                    act = 0.5 * hh * (1.0 + _erf(hh * (2.0 ** -0.5)))
                    w_ref[rows, cols] = (g[k] * act.astype(BF16).reshape(grp)).reshape(N_KEYS, LANES)
        vt = pltpu.bitcast(vt_ref[:, e0:e0 + ce], BF16)
        part = _dot(vt, w_ref[e0:e0 + ce, :])
        out = part if out is None else out + part
    acc_ref[...] += out

    @pl.when(j == pl.num_programs(1) - 1)
    def _():
        x = x1_ref[...] + acc_ref[...].T
        y_ref[...] = x * lax.rsqrt(jnp.mean(x * x, axis=-1, keepdims=True) + EPS) * gf_ref[...]


def _peer(xn, x1, e1, nb, rk, e2, u, vt, gf, tn, te=SUBLANES * N_KEYS):
    n = x1.shape[0]
    n_exp = 2 * u.shape[0]
    rspec = pl.BlockSpec((P_HEADS, N_KEYS, tn), lambda i, j: (0, 0, i))
    pspec = pl.BlockSpec((P_HEADS, N_KEYS // 2, tn), lambda i, j: (0, 0, i))
    return pl.pallas_call(
        _peer_kernel,
        grid=(n // tn, n_exp // te),
        in_specs=[
            pl.BlockSpec((tn // 2, D_MODEL), lambda i, j: (i, 0)),
            pl.BlockSpec((tn, D_MODEL), lambda i, j: (i, 0)),
            rspec, rspec, pspec, pspec,
            pl.BlockSpec((te // 2, D_MODEL), lambda i, j: (j, 0)),
            pl.BlockSpec((D_MODEL // 2, te), lambda i, j: (0, j)),
            pl.BlockSpec((1, D_MODEL), lambda i, j: (0, 0)),
        ],
        out_specs=pl.BlockSpec((tn, D_MODEL), lambda i, j: (i, 0)),
        out_shape=jax.ShapeDtypeStruct((n, D_MODEL), F32),
        scratch_shapes=[pltpu.VMEM((D_MODEL, tn), F32), pltpu.VMEM((te, tn), BF16)],
        compiler_params=_cparams(("parallel", "arbitrary")),
    )(xn, x1, e1, nb, rk, e2, u, vt, gf)


def _tile(n, pref):
    return pref if n % pref == 0 else n


def _layer(x, pos, s_ret, s_c, s_n, s_m, s_conv, wts):
    b, t, _ = x.shape
    n = b * t
    x2d = x.reshape(n, D_MODEL)
    proj, ifc, ifr = _proj(x2d, wts["norm1_g"], wts["w_main"], wts["wif_col"], wts["wif_row"], _tile(n, 1024))
    proj3 = proj.reshape(b, t, PROJ_MAIN)
    ifc3 = ifc.reshape(b, t, LANES)
    ifr3 = ifr.reshape(SUBLANES, b, t).transpose(1, 0, 2)
    y_r, ret_new = _retention(proj3, s_ret, pos, wts["ret_norm_g"])
    h_m, c_new, n_new, m_new = _mlstm(proj3, ifc3, ifr3, s_conv, s_c, s_n, s_m, wts["b_i"], wts["b_f"],
                                      wts["conv_w"], wts["conv_b"], wts["w_mq"], wts["w_mk"], wts["mlstm_norm_g"])
    conv_new = proj3[:, t - (CONV_W - 1):, 3 * D_MODEL:4 * D_MODEL]
    x1, xn2 = _merge(x2d, y_r.reshape(n, D_MODEL), h_m.reshape(n, D_MODEL), proj,
                     wts["w_br"], wts["w_bm"], wts["w_out"], wts["norm2_g"], _tile(n, 512))
    tn = _tile(n, 512)
    e1, nb, rk, e2 = _route(xn2, wts["w_pq_t"], wts["k1"], wts["k2"], tn)
    y = _peer(xn2, x1, e1, nb, rk, e2, wts["u"], wts["vt"], wts["norm_f_g"], tn)
    return (y.reshape(b, t, D_MODEL), ret_new, c_new, n_new.reshape(b, H_M, DH_M), m_new[:, :, 0, 0], conv_new)


def kernel(x_prompt, x_sample, state_ret, state_mlstm_C, state_mlstm_n, state_mlstm_m, state_conv, norm1_g, w_in, b_i, b_f, conv_w, conv_b, w_mq, w_mk, ret_norm_g, mlstm_norm_g, w_br, w_bm, w_out, norm2_g, w_pq, sub_keys1, sub_keys2, expert_u, expert_v, norm_f_g):
    assert w_in.shape[0] == 1, "single-layer step"
    bp, tp, _ = x_prompt.shape
    ts = x_sample.shape[1]
    qk, vr, wm = H_R * DK_R, H_R * DV_R, H_M * DH_M
    w = w_in[0]
    o_if = 2 * qk + 2 * vr + 3 * wm
    w_main = jnp.concatenate([w[:, :o_if], w[:, o_if + 2 * H_M:]], axis=1).astype(BF16)
    w_if = w[:, o_if:o_if + 2 * H_M]
    wts = dict(
        norm1_g=norm1_g[0].reshape(1, D_MODEL),
        w_main=w_main,
        wif_col=jnp.zeros((D_MODEL, LANES), F32).at[:, :2 * H_M].set(w_if),
        wif_row=w_if.T,
        b_i=b_i[0], b_f=b_f[0], conv_w=conv_w[0], conv_b=conv_b[0], w_mq=w_mq[0], w_mk=w_mk[0],
        ret_norm_g=ret_norm_g[0], mlstm_norm_g=mlstm_norm_g[0],
        w_br=w_br[0].astype(BF16), w_bm=w_bm[0].astype(BF16), w_out=w_out[0].astype(BF16),
        norm2_g=norm2_g[0].reshape(1, D_MODEL),
        w_pq_t=w_pq[0].T.astype(BF16),
        k1=sub_keys1[0].astype(BF16), k2=sub_keys2[0].astype(BF16),
        u=_pack_bf16(expert_u[0], False), vt=_pack_bf16(expert_v[0], True),
        norm_f_g=norm_f_g.reshape(1, D_MODEL),
    )
    pos_p = jnp.arange(tp, dtype=F32)
    pos_s = PAST_LEN + jnp.arange(ts, dtype=F32)
    zp = lambda *s: jnp.zeros((bp,) + s, F32)
    yp, r0, c0, n0, m0, v0 = _layer(x_prompt, pos_p, zp(H_R, DK_R, DV_R), zp(H_M, DH_M, DH_M), zp(H_M, DH_M),
                                    zp(H_M), zp(CONV_W - 1, wm), wts)
    ys, r1, c1, n1, m1, v1 = _layer(x_sample, pos_s, state_ret[0], state_mlstm_C[0], state_mlstm_n[0],
                                    state_mlstm_m[0], state_conv[0], wts)
    return (yp, ys, r0[None], c0[None], n0[None], m0[None], v0[None],
            r1[None], c1[None], n1[None], m1[None], v1[None])
```

```python
import math

import jax
import jax.numpy as jnp
from jax import lax
from jax.experimental import pallas as pl
from jax.experimental.pallas import tpu as pltpu

F32 = jnp.float32
BF16 = jnp.bfloat16
HIGHEST = lax.Precision.HIGHEST

D_MODEL = 1024
H_R, DK_R, DV_R = 4, 128, 256
H_M, DH_M = 4, 256
CONV_W = 4
N_KEYS = 128
P_HEADS = 8
P_HALF = 128
P_TOPK = 16
CHUNK = 128
PAST_LEN = 16384
ROPE_BASE = 10000.0
EPS = 1e-6
LANES = 128
SUBLANES = 8
VMEM_LIMIT = 56 * 1024 * 1024
PROJ_MAIN = 8192
NEG_INF = float("-inf")


def _cparams(sem):
    return pltpu.CompilerParams(dimension_semantics=sem, vmem_limit_bytes=VMEM_LIMIT)


def _dot(a, b):
    return jnp.dot(a, b, preferred_element_type=F32)


def _dot_nt(a, b):
    return lax.dot_general(a, b, (((1,), (1,)), ((), ())), preferred_element_type=F32)


def _dot_tn(a, b):
    return lax.dot_general(a, b, (((0,), (0,)), ((), ())), preferred_element_type=F32)


def _sigmoid(x):
    return 1.0 / (1.0 + jnp.exp(-x))


def _log_sigmoid(x):
    return jnp.minimum(x, 0.0) - jnp.log(1.0 + jnp.exp(-jnp.abs(x)))


def _head_norm(x, g):
    mu = jnp.mean(x, axis=-1, keepdims=True)
    xc = x - mu
    var = jnp.mean(xc * xc, axis=-1, keepdims=True)
    return xc * lax.rsqrt(var + EPS) * g


def _proj_kernel(x_ref, g_ref, w_ref, wifc_ref, wifr_ref, o_ref, oc_ref, or_ref, xn_ref):
    @pl.when(pl.program_id(1) == 0)
    def _():
        x = x_ref[...]
        xn = x * lax.rsqrt(jnp.mean(x * x, axis=-1, keepdims=True) + EPS) * g_ref[...]
        xh = xn.astype(BF16)
        xn_ref[...] = xh
        xl = (xn - xh.astype(F32)).astype(BF16)
        x3 = jnp.concatenate([xh, xl, xh], axis=1)
        oc_ref[...] = _dot(x3, wifc_ref[...])
        or_ref[...] = _dot_nt(wifr_ref[...], x3)

    o_ref[...] = _dot(xn_ref[...], w_ref[...])


def _split3(w, axis):
    hi = w.astype(BF16)
    lo = (w - hi.astype(F32)).astype(BF16)
    return jnp.concatenate([hi, hi, lo], axis=axis)


def _proj(x2d, g, w_main, wif_col, wif_row, tm, tn=1024):
    n = x2d.shape[0]
    wif_col, wif_row = _split3(wif_col, 0), _split3(wif_row, 1)
    return pl.pallas_call(
        _proj_kernel,
        grid=(n // tm, PROJ_MAIN // tn),
        in_specs=[
            pl.BlockSpec((tm, D_MODEL), lambda i, j: (i, 0)),
            pl.BlockSpec((1, D_MODEL), lambda i, j: (0, 0)),
            pl.BlockSpec((D_MODEL, tn), lambda i, j: (0, j)),
            pl.BlockSpec((3 * D_MODEL, LANES), lambda i, j: (0, 0)),
            pl.BlockSpec((SUBLANES, 3 * D_MODEL), lambda i, j: (0, 0)),
        ],
        out_specs=[
            pl.BlockSpec((tm, tn), lambda i, j: (i, j)),
            pl.BlockSpec((tm, LANES), lambda i, j: (i, 0)),
            pl.BlockSpec((SUBLANES, tm), lambda i, j: (0, i)),
        ],
        out_shape=[
            jax.ShapeDtypeStruct((n, PROJ_MAIN), F32),
            jax.ShapeDtypeStruct((n, LANES), F32),
            jax.ShapeDtypeStruct((SUBLANES, n), F32),
        ],
        scratch_shapes=[pltpu.VMEM((tm, D_MODEL), BF16)],
        compiler_params=_cparams(("parallel", "arbitrary")),
    )(x2d, g, w_main, wif_col, wif_row)


def _batch_block(b, nc, most):
    if nc > 1:
        return 1
    return max(nb for nb in (1, 2, 4, 8) if nb <= most and b % nb == 0)


def _ret_kernel(cdec_ref, q_ref, k_ref, v_ref, g_ref, cq_ref, sq_ref, ck_ref, sk_ref,
                dmat_ref, qdec_ref, kdec_ref, ng_ref, s0_ref, y_ref, s_ref):
    @pl.when(pl.program_id(1) == 0)
    def _():
        s_ref[...] = s0_ref[...]

    cq, sq, ck, sk = cq_ref[...], sq_ref[...], ck_ref[...], sk_ref[...]
    for bb in range(q_ref.shape[0]):
        for h in range(H_R):
            q = q_ref[bb, :, h * DK_R:(h + 1) * DK_R]
            k = k_ref[bb, :, h * DK_R:(h + 1) * DK_R]
            v = v_ref[bb, :, h * DV_R:(h + 1) * DV_R].astype(BF16)
            qr = (q * cq + pltpu.roll(q, DK_R // 2, 1) * sq).astype(BF16)
            kr = k * ck + pltpu.roll(k, DK_R // 2, 1) * sk
            s_old = s_ref[bb, h]
            sc = _dot_nt(qr, kr.astype(BF16)) * dmat_ref[h]
            inner = _dot(sc.astype(BF16), v)
            cross = _dot(qr, s_old.astype(BF16)) * qdec_ref[h]
            s_ref[bb, h] = cdec_ref[h] * s_old + _dot_tn((kr * kdec_ref[h]).astype(BF16), v)
            y = _head_norm(inner + cross, ng_ref[h])
            gate = g_ref[bb, :, h * DV_R:(h + 1) * DV_R]
            y_ref[bb, :, h * DV_R:(h + 1) * DV_R] = y * (gate * _sigmoid(gate))


def _retention(proj3, s0, pos, ret_norm_g):
    b, t, _ = proj3.shape
    l = CHUNK if t % CHUNK == 0 else t
    nc = t // l
    half = DK_R // 2
    freqs = jnp.exp(-math.log(ROPE_BASE) * jnp.arange(half, dtype=F32) / half)
    ang = pos[:, None] * freqs[None, :]
    cos, sin = jnp.cos(ang), jnp.sin(ang)
    cos2 = jnp.concatenate([cos, cos], axis=1)
    sin2 = jnp.concatenate([-sin, sin], axis=1)
    scale = DK_R ** -0.5
    log_gamma = jnp.log1p(-jnp.exp2(-5.0 - jnp.arange(H_R, dtype=F32)))
    idx = jnp.arange(l, dtype=F32)
    diff = idx[:, None] - idx[None, :]
    causal = diff >= 0
    dmat = jnp.where(causal[None], jnp.exp(log_gamma[:, None, None] * jnp.where(causal, diff, 0.0)[None]), 0.0)
    qdec = jnp.exp(log_gamma[:, None] * (idx[None, :] + 1.0))[:, :, None]
    kdec = jnp.exp(log_gamma[:, None] * (l - 1.0 - idx[None, :]))[:, :, None]
    cdec = jnp.exp(log_gamma * l)
    tab = pl.BlockSpec((l, DK_R), lambda i, c: (c, 0))
    nb = _batch_block(b, nc, 4)
    return pl.pallas_call(
        _ret_kernel,
        grid=(b // nb, nc),
        in_specs=[
            pl.BlockSpec(memory_space=pltpu.SMEM),
            pl.BlockSpec((nb, l, H_R * DK_R), lambda i, c: (i, c, 0)),
            pl.BlockSpec((nb, l, H_R * DK_R), lambda i, c: (i, c, 1)),
            pl.BlockSpec((nb, l, H_R * DV_R), lambda i, c: (i, c, 1)),
            pl.BlockSpec((nb, l, H_R * DV_R), lambda i, c: (i, c, 2)),
            tab, tab, tab, tab,
            pl.BlockSpec((H_R, l, l), lambda i, c: (0, 0, 0)),
            pl.BlockSpec((H_R, l, 1), lambda i, c: (0, 0, 0)),
            pl.BlockSpec((H_R, l, 1), lambda i, c: (0, 0, 0)),
            pl.BlockSpec((H_R, 1, DV_R), lambda i, c: (0, 0, 0)),
            pl.BlockSpec((nb, H_R, DK_R, DV_R), lambda i, c: (i, 0, 0, 0)),
        ],
        out_specs=[
            pl.BlockSpec((nb, l, H_R * DV_R), lambda i, c: (i, c, 0)),
            pl.BlockSpec((nb, H_R, DK_R, DV_R), lambda i, c: (i, 0, 0, 0)),
        ],
        out_shape=[
            jax.ShapeDtypeStruct((b, t, H_R * DV_R), F32),
            jax.ShapeDtypeStruct((b, H_R, DK_R, DV_R), F32),
        ],
        compiler_params=_cparams(("parallel", "arbitrary")),
    )(cdec, proj3, proj3, proj3, proj3, cos2 * scale, sin2 * scale, cos2, sin2,
      dmat, qdec, kdec, ret_norm_g.reshape(H_R, 1, DV_R), s0)


def _mlstm_kernel(x_ref, v_ref, o_ref, ifc_ref, ifr_ref, bc_ref, br_ref, cw_ref, cb_ref,
                  wq_ref, wk_ref, ng_ref, conv0_ref, c0_ref, n0_ref, m0_ref,
                  h_ref, c_ref, n_ref, m_ref, xp_ref):
    l = x_ref.shape[1]

    @pl.when(pl.program_id(1) == 0)
    def _():
        xp_ref[:, 0:SUBLANES, :] = conv0_ref[...]
        c_ref[...] = c0_ref[...]
        n_ref[...] = n0_ref[...]
        m_ref[...] = m0_ref[...]

    ti = lax.broadcasted_iota(jnp.int32, (l, l), 0)
    si = lax.broadcasted_iota(jnp.int32, (l, l), 1)
    causal = si <= ti
    tri = causal.astype(F32)
    for bb in range(x_ref.shape[0]):
        _mlstm_row(bb, l, causal, tri, x_ref, v_ref, o_ref, ifc_ref, ifr_ref, bc_ref, br_ref, cw_ref, cb_ref,
                   wq_ref, wk_ref, ng_ref, h_ref, c_ref, n_ref, m_ref, xp_ref)


def _mlstm_row(bb, l, causal, tri, x_ref, v_ref, o_ref, ifc_ref, ifr_ref, bc_ref, br_ref, cw_ref, cb_ref,
               wq_ref, wk_ref, ng_ref, h_ref, c_ref, n_ref, m_ref, xp_ref):
    xp_ref[bb, SUBLANES:SUBLANES + l, :] = x_ref[bb]
    xc = cb_ref[...]
    for j in range(CONV_W):
        off = SUBLANES - (CONV_W - 1) + j
        xc = xc + xp_ref[bb, off:off + l, :] * cw_ref[j:j + 1, :]
    xp_ref[bb, 0:SUBLANES, :] = xp_ref[bb, l:l + SUBLANES, :]
    xs = xc * _sigmoid(xc)

    pre_c = ifc_ref[bb] + bc_ref[...]
    pre_r = ifr_ref[bb] + br_ref[...]
    b_c = jnp.dot(tri, _log_sigmoid(pre_c), preferred_element_type=F32, precision=HIGHEST)
    b_r = lax.dot_general(_log_sigmoid(pre_r), tri, (((1,), (1,)), ((), ())),
                          preferred_element_type=F32, precision=HIGHEST)

    scale = DH_M ** -0.5
    for h in range(H_M):
        xh = xs[:, h * DH_M:(h + 1) * DH_M].astype(BF16)
        q = _dot(xh, wq_ref[h]) * scale
        k = _dot(xh, wk_ref[h])
        v = v_ref[bb, :, h * DH_M:(h + 1) * DH_M].astype(BF16)
        qb = q.astype(BF16)
        c_old, n_old = c_ref[bb, h], n_ref[bb, h]
        m_old = m_ref[bb, h][:, 0:1]
        bcol = b_c[:, H_M + h:H_M + h + 1]
        rcol = pre_c[:, h:h + 1] - bcol
        rrow = pre_r[h:h + 1, :] - b_r[H_M + h:H_M + h + 1, :]
        logd = jnp.where(causal, bcol + rrow, NEG_INF)
        m_t = jnp.maximum(bcol + m_old, jnp.max(logd, axis=-1, keepdims=True))
        dw = jnp.exp(logd - m_t)
        w_prev = jnp.exp(bcol + m_old - m_t)
        sc = _dot_nt(qb, k.astype(BF16)) * dw
        num = _dot(sc.astype(BF16), v) + w_prev * _dot(qb, c_old.astype(BF16))
        den = jnp.sum(sc, axis=-1, keepdims=True) + w_prev * jnp.sum(q * n_old, axis=-1, keepdims=True)
        hh = num / jnp.maximum(jnp.abs(den), jnp.exp(-m_t))
        m_new = m_t[l - 1:l, :]
        b_last = bcol[l - 1:l, :]
        wk = jnp.exp(b_last + rcol - m_new)
        dec = jnp.exp(b_last + m_old - m_new)
        kw = k * wk
        c_ref[bb, h] = dec * c_old + _dot_tn(kw.astype(BF16), v)
        n_ref[bb, h] = dec * n_old + jnp.sum(kw, axis=0, keepdims=True)
        m_ref[bb, h] = jnp.broadcast_to(m_new, (1, LANES))
        og = o_ref[bb, :, h * DH_M:(h + 1) * DH_M]
        h_ref[bb, :, h * DH_M:(h + 1) * DH_M] = _head_norm(hh, ng_ref[h]) * _sigmoid(og)


def _mlstm(proj3, ifc3, ifr3, conv0, c0, n0, m0, b_i, b_f, conv_w, conv_b, w_mq, w_mk, norm_g):
    b, t, _ = proj3.shape
    l = CHUNK if t % CHUNK == 0 else t
    nc = t // l
    w = H_M * DH_M
    bias = jnp.concatenate([b_i, b_f]).astype(F32)
    bias_c = jnp.zeros((1, LANES), F32).at[0, :2 * H_M].set(bias)
    bias_r = bias.reshape(2 * H_M, 1)
    conv0p = jnp.concatenate([jnp.zeros((b, SUBLANES - (CONV_W - 1), w), F32), conv0], axis=1)
    const2 = lambda i, c: (0, 0)
    const3 = lambda i, c: (0, 0, 0)
    state4 = lambda i, c: (i, 0, 0, 0)
    nb = _batch_block(b, nc, 1)
    return pl.pallas_call(
        _mlstm_kernel,
        grid=(b // nb, nc),
        in_specs=[
            pl.BlockSpec((nb, l, w), lambda i, c: (i, c, 3)),
            pl.BlockSpec((nb, l, w), lambda i, c: (i, c, 4)),
            pl.BlockSpec((nb, l, w), lambda i, c: (i, c, 5)),
            pl.BlockSpec((nb, l, LANES), lambda i, c: (i, c, 0)),
            pl.BlockSpec((nb, SUBLANES, l), lambda i, c: (i, 0, c)),
            pl.BlockSpec((1, LANES), const2),
            pl.BlockSpec((2 * H_M, 1), const2),
            pl.BlockSpec((CONV_W, w), const2),
            pl.BlockSpec((1, w), const2),
            pl.BlockSpec((H_M, DH_M, DH_M), const3),
            pl.BlockSpec((H_M, DH_M, DH_M), const3),
            pl.BlockSpec((H_M, 1, DH_M), const3),
            pl.BlockSpec((nb, SUBLANES, w), lambda i, c: (i, 0, 0)),
            pl.BlockSpec((nb, H_M, DH_M, DH_M), state4),
            pl.BlockSpec((nb, H_M, 1, DH_M), state4),
            pl.BlockSpec((nb, H_M, 1, LANES), state4),
        ],
        out_specs=[
            pl.BlockSpec((nb, l, w), lambda i, c: (i, c, 0)),
            pl.BlockSpec((nb, H_M, DH_M, DH_M), state4),
            pl.BlockSpec((nb, H_M, 1, DH_M), state4),
            pl.BlockSpec((nb, H_M, 1, LANES), state4),
        ],
        out_shape=[
            jax.ShapeDtypeStruct((b, t, w), F32),
            jax.ShapeDtypeStruct((b, H_M, DH_M, DH_M), F32),
            jax.ShapeDtypeStruct((b, H_M, 1, DH_M), F32),
            jax.ShapeDtypeStruct((b, H_M, 1, LANES), F32),
        ],
        scratch_shapes=[pltpu.VMEM((nb, l + SUBLANES, w), F32)],
        compiler_params=_cparams(("parallel", "arbitrary")),
    )(proj3, proj3, proj3, ifc3, ifr3, bias_c, bias_r, conv_w, conv_b.reshape(1, w),
      w_mq.astype(BF16), w_mk.astype(BF16), norm_g.reshape(H_M, 1, DH_M), conv0p,
      c0, n0.reshape(b, H_M, 1, DH_M), jnp.broadcast_to(m0[:, :, None, None], (b, H_M, 1, LANES)))


def _merge_kernel(x_ref, yr_ref, hm_ref, gr_ref, gm_ref, wbr_ref, wbm_ref, wo_ref, g2_ref, x1_ref, xn_ref):
    br = _dot(yr_ref[...].astype(BF16), wbr_ref[...])
    bm = _dot(hm_ref[...].astype(BF16), wbm_ref[...])
    merged = _sigmoid(gr_ref[...]) * br + _sigmoid(gm_ref[...]) * bm
    x1 = x_ref[...] + _dot(merged.astype(BF16), wo_ref[...])
    x1_ref[...] = x1
    xn = x1 * lax.rsqrt(jnp.mean(x1 * x1, axis=-1, keepdims=True) + EPS) * g2_ref[...]
    xn_ref[...] = xn.T.astype(BF16)


def _merge(x2d, yr, hm, proj, w_br, w_bm, w_out, g2, tm):
    n = x2d.shape[0]
    row = lambda i: (i, 0)
    const = lambda i: (0, 0)
    wspec = pl.BlockSpec((D_MODEL, D_MODEL), const)
    return pl.pallas_call(
        _merge_kernel,
        grid=(n // tm,),
        in_specs=[
            pl.BlockSpec((tm, D_MODEL), row),
            pl.BlockSpec((tm, D_MODEL), row),
            pl.BlockSpec((tm, D_MODEL), row),
            pl.BlockSpec((tm, D_MODEL), lambda i: (i, 6)),
            pl.BlockSpec((tm, D_MODEL), lambda i: (i, 7)),
            wspec, wspec, wspec,
            pl.BlockSpec((1, D_MODEL), const),
        ],
        out_specs=[pl.BlockSpec((tm, D_MODEL), row), pl.BlockSpec((D_MODEL, tm), lambda i: (0, i))],
        out_shape=[jax.ShapeDtypeStruct((n, D_MODEL), F32), jax.ShapeDtypeStruct((D_MODEL, n), BF16)],
        compiler_params=_cparams(("parallel",)),
    )(x2d, yr, hm, proj, proj, w_br, w_bm, w_out, g2)


def _top16(s):
    rows = lax.broadcasted_iota(jnp.int32, (P_TOPK, s.shape[1]), 0)
    top = jnp.zeros((P_TOPK, s.shape[1]), F32)
    rank = jnp.full(s.shape, float(P_TOPK), F32)
    for a in range(P_TOPK):
        m = jnp.max(s, axis=0, keepdims=True)
        hit = s == m
        top = jnp.where(rows == a, m, top)
        rank = jnp.where(hit, float(a), rank)
        s = jnp.where(hit, NEG_INF, s)
    return top, rank


def _bf16_pair(x):
    b = pltpu.bitcast(x.astype(BF16).astype(F32), jnp.uint32)
    return pltpu.bitcast(b | (b >> 16), jnp.int32)


def _route_kernel(xn_ref, wq_ref, k1_ref, k2_ref, e1_ref, nb_ref, rk_ref, e2_ref):
    tn = xn_ref.shape[1]
    qt = _dot(wq_ref[...], xn_ref[...])
    for h in range(P_HEADS):
        base = h * 2 * P_HALF
        s1f = _dot(k1_ref[...], qt[base:base + P_HALF].astype(BF16))
        s2f = _dot(k2_ref[...], qt[base + P_HALF:base + 2 * P_HALF].astype(BF16))
        for c in range(tn // LANES):
            cols = slice(c * LANES, (c + 1) * LANES)
            s1, s2 = s1f[:, cols], s2f[:, cols]
            v1, _ = _top16(s1)
            v2, rank2 = _top16(s2)
            cand = [v1 + v2[0:1]]
            cand += [v1[0:SUBLANES] + v2[b:b + 1] for b in range(1, SUBLANES)]
            cand += [v2[SUBLANES:P_TOPK] + v1[0:1]]
            cand = jnp.concatenate(cand, axis=0)
            cmax = v1[0:1] + v2[0:1]
            z = jnp.zeros_like(cmax)
            tau = cmax
            for _ in range(P_TOPK):
                tau = jnp.max(cand, axis=0, keepdims=True)
                z = z + jnp.exp(tau - cmax)
                cand = jnp.where(cand == tau, NEG_INF, cand)
            nb = jnp.zeros_like(s1)
            for b in range(P_TOPK):
                nb = jnp.where(s1 + v2[b:b + 1] >= tau, float(b + 1), nb)
            e1_ref[h, :, cols] = jnp.exp(s1 - v1[0:1])
            e2_ref[h, :, cols] = jnp.exp(s2 - v2[0:1]) / z
            nb_ref[h, :, cols] = _bf16_pair(nb)
            rk_ref[h, :, cols] = pltpu.bitcast(rank2.astype(BF16), jnp.int32)


def _route(xn, wq_t, k1, k2, tn):
    n = xn.shape[1]
    const = lambda i: (0, 0)
    ospec = pl.BlockSpec((P_HEADS, N_KEYS, tn), lambda i: (0, 0, i))
    pspec = pl.BlockSpec((P_HEADS, N_KEYS // 2, tn), lambda i: (0, 0, i))
    full = (P_HEADS, N_KEYS, n)
    return pl.pallas_call(
        _route_kernel,
        grid=(n // tn,),
        in_specs=[
            pl.BlockSpec((D_MODEL, tn), lambda i: (0, i)),
            pl.BlockSpec((P_HEADS * 2 * P_HALF, D_MODEL), const),
            pl.BlockSpec((N_KEYS, P_HALF), const),
            pl.BlockSpec((N_KEYS, P_HALF), const),
        ],
        out_specs=[ospec, ospec, pspec, ospec],
        out_shape=[jax.ShapeDtypeStruct(full, F32), jax.ShapeDtypeStruct(full, jnp.int32),
                   jax.ShapeDtypeStruct((P_HEADS, N_KEYS // 2, n), jnp.int32), jax.ShapeDtypeStruct(full, F32)],
        compiler_params=_cparams(("parallel",)),
    )(xn, wq_t, k1, k2)


def _erf(x):
    return lax.erf(x)


def _peer_gates(r_lo, r_hi, base, ht, w_ref, e1_ref, nb_ref, rk_ref, e2_ref):
    tn = w_ref.shape[1]
    pk = 2 * SUBLANES
    grp = (N_KEYS // pk, pk, LANES)
    r_share = 2
    for c0 in range(0, tn, LANES):
        cols = slice(c0, c0 + LANES)
        nb8 = [nb_ref[h, pl.ds(base, SUBLANES), cols] for h in range(P_HEADS)]
        e18 = [e1_ref[h, pl.ds(base, SUBLANES), cols] for h in range(P_HEADS)]
        for r0 in range(r_lo, r_hi, r_share):
            g = [jnp.zeros(grp, BF16) for _ in range(r_share)]
            for h in range(P_HEADS):
                rk = pltpu.bitcast(rk_ref[h, :, cols], BF16).reshape(grp)
                e2 = e2_ref[h, :, cols].astype(BF16).reshape(grp)
                for k in range(r_share):
                    r = r0 + k
                    nbb = pltpu.bitcast(jnp.broadcast_to(nb8[h][r:r + 1], (SUBLANES, LANES)), BF16)
                    e1b = jnp.broadcast_to(e18[h][r:r + 1], (pk, LANES)).astype(BF16)
                    g[k] = g[k] + e1b[None] * jnp.where(rk < nbb[None], e2, jnp.zeros_like(e2))
            for k in range(r_share):
                rows = slice((r0 + k - r_lo) * N_KEYS, (r0 + k - r_lo + 1) * N_KEYS)
                hh = ht[rows, cols]
                act = 0.5 * hh * (1.0 + _erf(hh * (2.0 ** -0.5)))
                w_ref[rows, cols] = (g[k] * act.astype(BF16).reshape(grp)).reshape(N_KEYS, LANES)


def _peer_kernel(xn_ref, x1_ref, e1_ref, nb_ref, rk_ref, e2_ref, u_ref, vt_ref, gf_ref, y_ref,
                 acc_ref, wa_ref, wb_ref, xs_ref):
    j = pl.program_id(1)
    te = u_ref.shape[0]
    ce = te // 2
    n_sub = te // N_KEYS
    assert n_sub == SUBLANES
    base = pl.multiple_of(j * n_sub, SUBLANES)
    routing = (e1_ref, nb_ref, rk_ref, e2_ref)

    @pl.when(j == 0)
    def _():
        acc_ref[...] = jnp.zeros_like(acc_ref)
        xs_ref[...] = xn_ref[...]

    xn = xs_ref[...]
    u = u_ref[...]
    hta = _dot(u[0:ce], xn)
    htb = _dot(u[ce:te], xn)
    _peer_gates(0, n_sub // 2, base, hta, wa_ref, *routing)
    _peer_gates(n_sub // 2, n_sub, base, htb, wb_ref, *routing)
    vt = vt_ref[...]
    acc_ref[...] += _dot(vt[:, 0:ce], wa_ref[...]) + _dot(vt[:, ce:te], wb_ref[...])

    @pl.when(j == pl.num_programs(1) - 1)
    def _():
        x = x1_ref[...] + acc_ref[...].T
        y_ref[...] = x * lax.rsqrt(jnp.mean(x * x, axis=-1, keepdims=True) + EPS) * gf_ref[...]


def _peer(xn, x1, e1, nb, rk, e2, u, vt, gf, tn, te=SUBLANES * N_KEYS):
    n = x1.shape[0]
    n_exp = u.shape[0]
    rspec = pl.BlockSpec((P_HEADS, N_KEYS, tn), lambda i, j: (0, 0, i))
    pspec = pl.BlockSpec((P_HEADS, N_KEYS // 2, tn), lambda i, j: (0, 0, i))
    return pl.pallas_call(
        _peer_kernel,
        grid=(n // tn, n_exp // te),
        in_specs=[
            pl.BlockSpec((D_MODEL, tn), lambda i, j: (0, i)),
            pl.BlockSpec((tn, D_MODEL), lambda i, j: (i, 0)),
            rspec, rspec, pspec, rspec,
            pl.BlockSpec((te, D_MODEL), lambda i, j: (j, 0)),
            pl.BlockSpec((D_MODEL, te), lambda i, j: (0, j)),
            pl.BlockSpec((1, D_MODEL), lambda i, j: (0, 0)),
        ],
        out_specs=pl.BlockSpec((tn, D_MODEL), lambda i, j: (i, 0)),
        out_shape=jax.ShapeDtypeStruct((n, D_MODEL), F32),
        scratch_shapes=[pltpu.VMEM((D_MODEL, tn), F32), pltpu.VMEM((te // 2, tn), BF16),
                        pltpu.VMEM((te // 2, tn), BF16), pltpu.VMEM((D_MODEL, tn), BF16)],
        compiler_params=_cparams(("parallel", "arbitrary")),
    )(xn, x1, e1, nb, rk, e2, u, vt, gf)


def _tile(n, pref):
    return pref if n % pref == 0 else n


def _layer(x, pos, s_ret, s_c, s_n, s_m, s_conv, wts):
    b, t, _ = x.shape
    n = b * t
    x2d = x.reshape(n, D_MODEL)
    proj, ifc, ifr = _proj(x2d, wts["norm1_g"], wts["w_main"], wts["wif_col"], wts["wif_row"], _tile(n, 1024))
    proj3 = proj.reshape(b, t, PROJ_MAIN)
    ifc3 = ifc.reshape(b, t, LANES)
    ifr3 = ifr.reshape(SUBLANES, b, t).transpose(1, 0, 2)
    y_r, ret_new = _retention(proj3, s_ret, pos, wts["ret_norm_g"])
    h_m, c_new, n_new, m_new = _mlstm(proj3, ifc3, ifr3, s_conv, s_c, s_n, s_m, wts["b_i"], wts["b_f"],
                                      wts["conv_w"], wts["conv_b"], wts["w_mq"], wts["w_mk"], wts["mlstm_norm_g"])
    conv_new = proj3[:, t - (CONV_W - 1):, 3 * D_MODEL:4 * D_MODEL]
    x1, xn2 = _merge(x2d, y_r.reshape(n, D_MODEL), h_m.reshape(n, D_MODEL), proj,
                     wts["w_br"], wts["w_bm"], wts["w_out"], wts["norm2_g"], _tile(n, 512))
    tn = _tile(n, 512)
    e1, nb, rk, e2 = _route(xn2, wts["w_pq_t"], wts["k1"], wts["k2"], tn)
    y = _peer(xn2, x1, e1, nb, rk, e2, wts["u"], wts["vt"], wts["norm_f_g"], tn)
    return (y.reshape(b, t, D_MODEL), ret_new, c_new, n_new.reshape(b, H_M, DH_M), m_new[:, :, 0, 0], conv_new)


def kernel(x_prompt, x_sample, state_ret, state_mlstm_C, state_mlstm_n, state_mlstm_m, state_conv, norm1_g, w_in, b_i, b_f, conv_w, conv_b, w_mq, w_mk, ret_norm_g, mlstm_norm_g, w_br, w_bm, w_out, norm2_g, w_pq, sub_keys1, sub_keys2, expert_u, expert_v, norm_f_g):
    assert w_in.shape[0] == 1, "single-layer step"
    bp, tp, _ = x_prompt.shape
    ts = x_sample.shape[1]
    qk, vr, wm = H_R * DK_R, H_R * DV_R, H_M * DH_M
    w = w_in[0]
    o_if = 2 * qk + 2 * vr + 3 * wm
    w_main = jnp.concatenate([w[:, :o_if], w[:, o_if + 2 * H_M:]], axis=1).astype(BF16)
    w_if = w[:, o_if:o_if + 2 * H_M]
    wts = dict(
        norm1_g=norm1_g[0].reshape(1, D_MODEL),
        w_main=w_main,
        wif_col=jnp.zeros((D_MODEL, LANES), F32).at[:, :2 * H_M].set(w_if),
        wif_row=w_if.T,
        b_i=b_i[0], b_f=b_f[0], conv_w=conv_w[0], conv_b=conv_b[0], w_mq=w_mq[0], w_mk=w_mk[0],
        ret_norm_g=ret_norm_g[0], mlstm_norm_g=mlstm_norm_g[0],
        w_br=w_br[0].astype(BF16), w_bm=w_bm[0].astype(BF16), w_out=w_out[0].astype(BF16),
        norm2_g=norm2_g[0].reshape(1, D_MODEL),
        w_pq_t=w_pq[0].T.astype(BF16),
        k1=sub_keys1[0].astype(BF16), k2=sub_keys2[0].astype(BF16),
        u=expert_u[0].astype(BF16), vt=expert_v[0].T.astype(BF16),
        norm_f_g=norm_f_g.reshape(1, D_MODEL),
    )
    pos_p = jnp.arange(tp, dtype=F32)
    pos_s = PAST_LEN + jnp.arange(ts, dtype=F32)
    zp = lambda *s: jnp.zeros((bp,) + s, F32)
    yp, r0, c0, n0, m0, v0 = _layer(x_prompt, pos_p, zp(H_R, DK_R, DV_R), zp(H_M, DH_M, DH_M), zp(H_M, DH_M),
                                    zp(H_M), zp(CONV_W - 1, wm), wts)
    ys, r1, c1, n1, m1, v1 = _layer(x_sample, pos_s, state_ret[0], state_mlstm_C[0], state_mlstm_n[0],
                                    state_mlstm_m[0], state_conv[0], wts)
    return (yp, ys, r0[None], c0[None], n0[None], m0[None], v0[None],
            r1[None], c1[None], n1[None], m1[None], v1[None])
```

```python
import math

import jax
import jax.numpy as jnp
from jax import lax
from jax.experimental import pallas as pl
from jax.experimental.pallas import tpu as pltpu

F32 = jnp.float32
BF16 = jnp.bfloat16
HIGHEST = lax.Precision.HIGHEST

D_MODEL = 1024
H_R, DK_R, DV_R = 4, 128, 256
H_M, DH_M = 4, 256
CONV_W = 4
N_KEYS = 128
P_HEADS = 8
P_HALF = 128
P_TOPK = 16
CHUNK = 128
PAST_LEN = 16384
ROPE_BASE = 10000.0
EPS = 1e-6
LANES = 128
SUBLANES = 8
VMEM_LIMIT = 56 * 1024 * 1024
PROJ_MAIN = 8192
NEG_INF = float("-inf")
GELU_C = 2.0 ** -0.5


def _cparams(sem):
    return pltpu.CompilerParams(dimension_semantics=sem, vmem_limit_bytes=VMEM_LIMIT)


def _dot(a, b):
    return jnp.dot(a, b, preferred_element_type=F32)


def _dot_nt(a, b):
    return lax.dot_general(a, b, (((1,), (1,)), ((), ())), preferred_element_type=F32)


def _dot_tn(a, b):
    return lax.dot_general(a, b, (((0,), (0,)), ((), ())), preferred_element_type=F32)


def _sigmoid(x):
    return 1.0 / (1.0 + jnp.exp(-x))


def _log_sigmoid(x):
    return jnp.minimum(x, 0.0) - jnp.log(1.0 + jnp.exp(-jnp.abs(x)))


def _head_norm(x, g):
    mu = jnp.mean(x, axis=-1, keepdims=True)
    xc = x - mu
    var = jnp.mean(xc * xc, axis=-1, keepdims=True)
    return xc * lax.rsqrt(var + EPS) * g


def _proj_kernel(x_ref, g_ref, w_ref, wifc_ref, o_ref, oc_ref, or_ref, xn_ref):
    @pl.when(pl.program_id(1) == 0)
    def _():
        x = x_ref[...]
        xn = x * lax.rsqrt(jnp.mean(x * x, axis=-1, keepdims=True) + EPS) * g_ref[...]
        xh = xn.astype(BF16)
        xn_ref[...] = xh
        xl = (xn - xh.astype(F32)).astype(BF16)
        x3 = jnp.concatenate([xh, xl, xh], axis=1)
        oc = _dot(x3, wifc_ref[...])
        oc_ref[...] = oc
        or_ref[...] = oc.T[0:SUBLANES, :]

    o_ref[...] = _dot(xn_ref[...], w_ref[...])


def _split3(w, axis):
    hi = w.astype(BF16)
    lo = (w - hi.astype(F32)).astype(BF16)
    return jnp.concatenate([hi, hi, lo], axis=axis)


def _proj(x2d, g, w_main, wif_col, tm, tn=1024):
    n = x2d.shape[0]
    wif_col = _split3(wif_col, 0)
    return pl.pallas_call(
        _proj_kernel,
        grid=(n // tm, PROJ_MAIN // tn),
        in_specs=[
            pl.BlockSpec((tm, D_MODEL), lambda i, j: (i, 0)),
            pl.BlockSpec((1, D_MODEL), lambda i, j: (0, 0)),
            pl.BlockSpec((D_MODEL, tn), lambda i, j: (0, j)),
            pl.BlockSpec((3 * D_MODEL, LANES), lambda i, j: (0, 0)),
        ],
        out_specs=[
            pl.BlockSpec((tm, tn), lambda i, j: (i, j)),
            pl.BlockSpec((tm, LANES), lambda i, j: (i, 0)),
            pl.BlockSpec((SUBLANES, tm), lambda i, j: (0, i)),
        ],
        out_shape=[
            jax.ShapeDtypeStruct((n, PROJ_MAIN), F32),
            jax.ShapeDtypeStruct((n, LANES), F32),
            jax.ShapeDtypeStruct((SUBLANES, n), F32),
        ],
        scratch_shapes=[pltpu.VMEM((tm, D_MODEL), BF16)],
        compiler_params=_cparams(("parallel", "arbitrary")),
    )(x2d, g, w_main, wif_col)


def _batch_block(b, nc, most):
    if nc > 1:
        return 1
    return max(nb for nb in (1, 2, 4, 8) if nb <= most and b % nb == 0)


def _ret_kernel(cdec_ref, q_ref, k_ref, v_ref, g_ref, cq_ref, sq_ref, ck_ref, sk_ref,
                dmat_ref, qdec_ref, kdec_ref, ng_ref, s0_ref, y_ref, s_ref):
    @pl.when(pl.program_id(1) == 0)
    def _():
        s_ref[...] = s0_ref[...]

    cq, sq, ck, sk = cq_ref[...], sq_ref[...], ck_ref[...], sk_ref[...]
    for bb in range(q_ref.shape[0]):
        for h in range(H_R):
            q = q_ref[bb, :, h * DK_R:(h + 1) * DK_R]
            k = k_ref[bb, :, h * DK_R:(h + 1) * DK_R]
            v = v_ref[bb, :, h * DV_R:(h + 1) * DV_R].astype(BF16)
            qr = (q * cq + pltpu.roll(q, DK_R // 2, 1) * sq).astype(BF16)
            kr = k * ck + pltpu.roll(k, DK_R // 2, 1) * sk
            s_old = s_ref[bb, h]
            sc = _dot_nt(qr, kr.astype(BF16)) * dmat_ref[h]
            inner = _dot(sc.astype(BF16), v)
            cross = _dot(qr, s_old.astype(BF16)) * qdec_ref[h]
            s_ref[bb, h] = cdec_ref[h] * s_old + _dot_tn((kr * kdec_ref[h]).astype(BF16), v)
            y = _head_norm(inner + cross, ng_ref[h])
            gate = g_ref[bb, :, h * DV_R:(h + 1) * DV_R]
            y_ref[bb, :, h * DV_R:(h + 1) * DV_R] = y * (gate * _sigmoid(gate))


def _retention(proj3, s0, pos, ret_norm_g):
    b, t, _ = proj3.shape
    l = CHUNK if t % CHUNK == 0 else t
    nc = t // l
    half = DK_R // 2
    freqs = jnp.exp(-math.log(ROPE_BASE) * jnp.arange(half, dtype=F32) / half)
    ang = pos[:, None] * freqs[None, :]
    cos, sin = jnp.cos(ang), jnp.sin(ang)
    cos2 = jnp.concatenate([cos, cos], axis=1)
    sin2 = jnp.concatenate([-sin, sin], axis=1)
    scale = DK_R ** -0.5
    log_gamma = jnp.log1p(-jnp.exp2(-5.0 - jnp.arange(H_R, dtype=F32)))
    idx = jnp.arange(l, dtype=F32)
    diff = idx[:, None] - idx[None, :]
    causal = diff >= 0
    dmat = jnp.where(causal[None], jnp.exp(log_gamma[:, None, None] * jnp.where(causal, diff, 0.0)[None]), 0.0)
    qdec = jnp.exp(log_gamma[:, None] * (idx[None, :] + 1.0))[:, :, None]
    kdec = jnp.exp(log_gamma[:, None] * (l - 1.0 - idx[None, :]))[:, :, None]
    cdec = jnp.exp(log_gamma * l)
    tab = pl.BlockSpec((l, DK_R), lambda i, c: (c, 0))
    nb = _batch_block(b, nc, 8)
    return pl.pallas_call(
        _ret_kernel,
        grid=(b // nb, nc),
        in_specs=[
            pl.BlockSpec(memory_space=pltpu.SMEM),
            pl.BlockSpec((nb, l, H_R * DK_R), lambda i, c: (i, c, 0)),
            pl.BlockSpec((nb, l, H_R * DK_R), lambda i, c: (i, c, 1)),
            pl.BlockSpec((nb, l, H_R * DV_R), lambda i, c: (i, c, 1)),
            pl.BlockSpec((nb, l, H_R * DV_R), lambda i, c: (i, c, 2)),
            tab, tab, tab, tab,
            pl.BlockSpec((H_R, l, l), lambda i, c: (0, 0, 0)),
            pl.BlockSpec((H_R, l, 1), lambda i, c: (0, 0, 0)),
            pl.BlockSpec((H_R, l, 1), lambda i, c: (0, 0, 0)),
            pl.BlockSpec((H_R, 1, DV_R), lambda i, c: (0, 0, 0)),
            pl.BlockSpec((nb, H_R, DK_R, DV_R), lambda i, c: (i, 0, 0, 0)),
        ],
        out_specs=[
            pl.BlockSpec((nb, l, H_R * DV_R), lambda i, c: (i, c, 0)),
            pl.BlockSpec((nb, H_R, DK_R, DV_R), lambda i, c: (i, 0, 0, 0)),
        ],
        out_shape=[
            jax.ShapeDtypeStruct((b, t, H_R * DV_R), F32),
            jax.ShapeDtypeStruct((b, H_R, DK_R, DV_R), F32),
        ],
        compiler_params=_cparams(("parallel", "arbitrary")),
    )(cdec, proj3, proj3, proj3, proj3, cos2 * scale, sin2 * scale, cos2, sin2,
      dmat, qdec, kdec, ret_norm_g.reshape(H_R, 1, DV_R), s0)


def _mlstm_kernel(x_ref, v_ref, o_ref, ifc_ref, ifr_ref, bc_ref, br_ref, cw_ref, cb_ref,
                  wq_ref, wk_ref, ng_ref, conv0_ref, c0_ref, n0_ref, m0_ref,
                  h_ref, c_ref, n_ref, m_ref, xp_ref):
    l = x_ref.shape[1]

    @pl.when(pl.program_id(1) == 0)
    def _():
        xp_ref[:, 0:SUBLANES, :] = conv0_ref[...]
        c_ref[...] = c0_ref[...]
        n_ref[...] = n0_ref[...]
        m_ref[...] = m0_ref[...]

    ti = lax.broadcasted_iota(jnp.int32, (l, l), 0)
    si = lax.broadcasted_iota(jnp.int32, (l, l), 1)
    causal = si <= ti
    tri = causal.astype(F32)
    for bb in range(x_ref.shape[0]):
        _mlstm_row(bb, l, causal, tri, x_ref, v_ref, o_ref, ifc_ref, ifr_ref, bc_ref, br_ref, cw_ref, cb_ref,
                   wq_ref, wk_ref, ng_ref, h_ref, c_ref, n_ref, m_ref, xp_ref)


def _mlstm_row(bb, l, causal, tri, x_ref, v_ref, o_ref, ifc_ref, ifr_ref, bc_ref, br_ref, cw_ref, cb_ref,
               wq_ref, wk_ref, ng_ref, h_ref, c_ref, n_ref, m_ref, xp_ref):
    xp_ref[bb, SUBLANES:SUBLANES + l, :] = x_ref[bb]
    xc = cb_ref[...]
    for j in range(CONV_W):
        off = SUBLANES - (CONV_W - 1) + j
        xc = xc + xp_ref[bb, off:off + l, :] * cw_ref[j:j + 1, :]
    xp_ref[bb, 0:SUBLANES, :] = xp_ref[bb, l:l + SUBLANES, :]
    xs = xc * _sigmoid(xc)

    pre_c = ifc_ref[bb] + bc_ref[...]
    pre_r = ifr_ref[bb] + br_ref[...]
    b_c = jnp.dot(tri, _log_sigmoid(pre_c), preferred_element_type=F32, precision=HIGHEST)
    b_r = lax.dot_general(_log_sigmoid(pre_r), tri, (((1,), (1,)), ((), ())),
                          preferred_element_type=F32, precision=HIGHEST)

    scale = DH_M ** -0.5
    for h in range(H_M):
        xh = xs[:, h * DH_M:(h + 1) * DH_M].astype(BF16)
        q = _dot(xh, wq_ref[h]) * scale
        k = _dot(xh, wk_ref[h])
        v = v_ref[bb, :, h * DH_M:(h + 1) * DH_M].astype(BF16)
        qb = q.astype(BF16)
        c_old, n_old = c_ref[bb, h], n_ref[bb, h]
        m_old = m_ref[bb, h][:, 0:1]
        bcol = b_c[:, H_M + h:H_M + h + 1]
        rcol = pre_c[:, h:h + 1] - bcol
        rrow = pre_r[h:h + 1, :] - b_r[H_M + h:H_M + h + 1, :]
        logd = jnp.where(causal, bcol + rrow, NEG_INF)
        m_t = jnp.maximum(bcol + m_old, jnp.max(logd, axis=-1, keepdims=True))
        dw = jnp.exp(logd - m_t)
        w_prev = jnp.exp(bcol + m_old - m_t)
        sc = _dot_nt(qb, k.astype(BF16)) * dw
        num = _dot(sc.astype(BF16), v) + w_prev * _dot(qb, c_old.astype(BF16))
        den = jnp.sum(sc, axis=-1, keepdims=True) + w_prev * jnp.sum(q * n_old, axis=-1, keepdims=True)
        hh = num / jnp.maximum(jnp.abs(den), jnp.exp(-m_t))
        m_new = m_t[l - 1:l, :]
        b_last = bcol[l - 1:l, :]
        wk = jnp.exp(b_last + rcol - m_new)
        dec = jnp.exp(b_last + m_old - m_new)
        kw = k * wk
        c_ref[bb, h] = dec * c_old + _dot_tn(kw.astype(BF16), v)
        n_ref[bb, h] = dec * n_old + jnp.sum(kw, axis=0, keepdims=True)
        m_ref[bb, h] = jnp.broadcast_to(m_new, (1, LANES))
        og = o_ref[bb, :, h * DH_M:(h + 1) * DH_M]
        h_ref[bb, :, h * DH_M:(h + 1) * DH_M] = _head_norm(hh, ng_ref[h]) * _sigmoid(og)


def _mlstm(proj3, ifc3, ifr3, conv0, c0, n0, m0, b_i, b_f, conv_w, conv_b, w_mq, w_mk, norm_g):
    b, t, _ = proj3.shape
    l = CHUNK if t % CHUNK == 0 else t
    nc = t // l
    w = H_M * DH_M
    bias = jnp.concatenate([b_i, b_f]).astype(F32)
    bias_c = jnp.zeros((1, LANES), F32).at[0, :2 * H_M].set(bias)
    bias_r = bias.reshape(2 * H_M, 1)
    conv0p = jnp.concatenate([jnp.zeros((b, SUBLANES - (CONV_W - 1), w), F32), conv0], axis=1)
    const2 = lambda i, c: (0, 0)
    const3 = lambda i, c: (0, 0, 0)
    state4 = lambda i, c: (i, 0, 0, 0)
    nb = _batch_block(b, nc, 1)
    return pl.pallas_call(
        _mlstm_kernel,
        grid=(b // nb, nc),
        in_specs=[
            pl.BlockSpec((nb, l, w), lambda i, c: (i, c, 3)),
            pl.BlockSpec((nb, l, w), lambda i, c: (i, c, 4)),
            pl.BlockSpec((nb, l, w), lambda i, c: (i, c, 5)),
            pl.BlockSpec((nb, l, LANES), lambda i, c: (i, c, 0)),
            pl.BlockSpec((nb, SUBLANES, l), lambda i, c: (i, 0, c)),
            pl.BlockSpec((1, LANES), const2),
            pl.BlockSpec((2 * H_M, 1), const2),
            pl.BlockSpec((CONV_W, w), const2),
            pl.BlockSpec((1, w), const2),
            pl.BlockSpec((H_M, DH_M, DH_M), const3),
            pl.BlockSpec((H_M, DH_M, DH_M), const3),
            pl.BlockSpec((H_M, 1, DH_M), const3),
            pl.BlockSpec((nb, SUBLANES, w), lambda i, c: (i, 0, 0)),
            pl.BlockSpec((nb, H_M, DH_M, DH_M), state4),
            pl.BlockSpec((nb, H_M, 1, DH_M), state4),
            pl.BlockSpec((nb, H_M, 1, LANES), state4),
        ],
        out_specs=[
            pl.BlockSpec((nb, l, w), lambda i, c: (i, c, 0)),
            pl.BlockSpec((nb, H_M, DH_M, DH_M), state4),
            pl.BlockSpec((nb, H_M, 1, DH_M), state4),
            pl.BlockSpec((nb, H_M, 1, LANES), state4),
        ],
        out_shape=[
            jax.ShapeDtypeStruct((b, t, w), F32),
            jax.ShapeDtypeStruct((b, H_M, DH_M, DH_M), F32),
            jax.ShapeDtypeStruct((b, H_M, 1, DH_M), F32),
            jax.ShapeDtypeStruct((b, H_M, 1, LANES), F32),
        ],
        scratch_shapes=[pltpu.VMEM((nb, l + SUBLANES, w), F32)],
        compiler_params=_cparams(("parallel", "arbitrary")),
    )(proj3, proj3, proj3, ifc3, ifr3, bias_c, bias_r, conv_w, conv_b.reshape(1, w),
      w_mq.astype(BF16), w_mk.astype(BF16), norm_g.reshape(H_M, 1, DH_M), conv0p,
      c0, n0.reshape(b, H_M, 1, DH_M), jnp.broadcast_to(m0[:, :, None, None], (b, H_M, 1, LANES)))


def _merge_kernel(x_ref, yr_ref, hm_ref, gr_ref, gm_ref, wbr_ref, wbm_ref, wo_ref, g2_ref, x1_ref, xn_ref, xc_ref):
    br = _dot(yr_ref[...].astype(BF16), wbr_ref[...])
    bm = _dot(hm_ref[...].astype(BF16), wbm_ref[...])
    merged = _sigmoid(gr_ref[...]) * br + _sigmoid(gm_ref[...]) * bm
    x1 = x_ref[...] + _dot(merged.astype(BF16), wo_ref[...])
    x1_ref[...] = x1
    xn = x1 * lax.rsqrt(jnp.mean(x1 * x1, axis=-1, keepdims=True) + EPS) * g2_ref[...]
    xt = xn.T
    xn_ref[...] = xt.astype(BF16)
    xc_ref[...] = (xt * GELU_C).astype(BF16)


def _merge(x2d, yr, hm, proj, w_br, w_bm, w_out, g2, tm):
    n = x2d.shape[0]
    row = lambda i: (i, 0)
    const = lambda i: (0, 0)
    wspec = pl.BlockSpec((D_MODEL, D_MODEL), const)
    return pl.pallas_call(
        _merge_kernel,
        grid=(n // tm,),
        in_specs=[
            pl.BlockSpec((tm, D_MODEL), row),
            pl.BlockSpec((tm, D_MODEL), row),
            pl.BlockSpec((tm, D_MODEL), row),
            pl.BlockSpec((tm, D_MODEL), lambda i: (i, 6)),
            pl.BlockSpec((tm, D_MODEL), lambda i: (i, 7)),
            wspec, wspec, wspec,
            pl.BlockSpec((1, D_MODEL), const),
        ],
        out_specs=[pl.BlockSpec((tm, D_MODEL), row), pl.BlockSpec((D_MODEL, tm), lambda i: (0, i)),
                   pl.BlockSpec((D_MODEL, tm), lambda i: (0, i))],
        out_shape=[jax.ShapeDtypeStruct((n, D_MODEL), F32), jax.ShapeDtypeStruct((D_MODEL, n), BF16),
                   jax.ShapeDtypeStruct((D_MODEL, n), BF16)],
        compiler_params=_cparams(("parallel",)),
    )(x2d, yr, hm, proj, proj, w_br, w_bm, w_out, g2)


def _top16(s):
    rows = lax.broadcasted_iota(jnp.int32, (P_TOPK, s.shape[1]), 0)
    top = jnp.zeros((P_TOPK, s.shape[1]), F32)
    rank = jnp.full(s.shape, float(P_TOPK), F32)
    for a in range(P_TOPK):
        m = jnp.max(s, axis=0, keepdims=True)
        hit = s == m
        top = jnp.where(rows == a, m, top)
        rank = jnp.where(hit, float(a), rank)
        s = jnp.where(hit, NEG_INF, s)
    return top, rank


def _bf16_pair(x):
    b = pltpu.bitcast(x.astype(BF16).astype(F32), jnp.uint32)
    return pltpu.bitcast(b | (b >> 16), jnp.int32)


def _route_kernel(xn_ref, wq_ref, k1_ref, k2_ref, e1_ref, nb_ref, rk_ref, e2_ref):
    tn = xn_ref.shape[1]
    qt = _dot(wq_ref[...], xn_ref[...])
    for h in range(P_HEADS):
        base = h * 2 * P_HALF
        s1f = _dot(k1_ref[...], qt[base:base + P_HALF].astype(BF16))
        s2f = _dot(k2_ref[...], qt[base + P_HALF:base + 2 * P_HALF].astype(BF16))
        for c in range(tn // LANES):
            cols = slice(c * LANES, (c + 1) * LANES)
            s1, s2 = s1f[:, cols], s2f[:, cols]
            v1, _ = _top16(s1)
            v2, rank2 = _top16(s2)
            cand = [v1 + v2[0:1]]
            cand += [v1[0:SUBLANES] + v2[b:b + 1] for b in range(1, SUBLANES)]
            cand += [v2[SUBLANES:P_TOPK] + v1[0:1]]
            cand = jnp.concatenate(cand, axis=0)
            cmax = v1[0:1] + v2[0:1]
            z = jnp.zeros_like(cmax)
            tau = cmax
            for _ in range(P_TOPK):
                tau = jnp.max(cand, axis=0, keepdims=True)
                z = z + jnp.exp(tau - cmax)
                cand = jnp.where(cand == tau, NEG_INF, cand)
            nb = jnp.zeros_like(s1)
            for b in range(P_TOPK):
                nb = jnp.where(s1 + v2[b:b + 1] >= tau, float(b + 1), nb)
            e1_ref[h, :, cols] = jnp.exp(s1 - v1[0:1])
            e2_ref[h, :, cols] = jnp.exp(s2 - v2[0:1]) * (GELU_C / z)
            nb_ref[h, :, cols] = _bf16_pair(nb)
            rk_ref[h, :, cols] = pltpu.bitcast(rank2.astype(BF16), jnp.int32)


def _route(xn, wq_t, k1, k2, tn):
    n = xn.shape[1]
    const = lambda i: (0, 0)
    ospec = pl.BlockSpec((P_HEADS, N_KEYS, tn), lambda i: (0, 0, i))
    pspec = pl.BlockSpec((P_HEADS, N_KEYS // 2, tn), lambda i: (0, 0, i))
    full = (P_HEADS, N_KEYS, n)
    return pl.pallas_call(
        _route_kernel,
        grid=(n // tn,),
        in_specs=[
            pl.BlockSpec((D_MODEL, tn), lambda i: (0, i)),
            pl.BlockSpec((P_HEADS * 2 * P_HALF, D_MODEL), const),
            pl.BlockSpec((N_KEYS, P_HALF), const),
            pl.BlockSpec((N_KEYS, P_HALF), const),
        ],
        out_specs=[ospec, ospec, pspec, ospec],
        out_shape=[jax.ShapeDtypeStruct(full, F32), jax.ShapeDtypeStruct(full, jnp.int32),
                   jax.ShapeDtypeStruct((P_HEADS, N_KEYS // 2, n), jnp.int32), jax.ShapeDtypeStruct(full, F32)],
        compiler_params=_cparams(("parallel",)),
    )(xn, wq_t, k1, k2)


def _erf(x):
    return lax.erf(x)


def _peer_gates(r_lo, r_hi, base, ht, w_ref, e1_ref, nb_ref, rk_ref, e2_ref):
    tn = w_ref.shape[1]
    pk = 2 * SUBLANES
    grp = (N_KEYS // pk, pk, LANES)
    r_share = 2
    for c0 in range(0, tn, LANES):
        cols = slice(c0, c0 + LANES)
        nb8 = [nb_ref[h, pl.ds(base, SUBLANES), cols] for h in range(P_HEADS)]
        e18 = [e1_ref[h, pl.ds(base, SUBLANES), cols] for h in range(P_HEADS)]
        for r0 in range(r_lo, r_hi, r_share):
            g = [jnp.zeros(grp, BF16) for _ in range(r_share)]
            for h in range(P_HEADS):
                rk = pltpu.bitcast(rk_ref[h, :, cols], BF16).reshape(grp)
                e2 = e2_ref[h, :, cols].astype(BF16).reshape(grp)
                for k in range(r_share):
                    r = r0 + k
                    nbb = pltpu.bitcast(jnp.broadcast_to(nb8[h][r:r + 1], (SUBLANES, LANES)), BF16)
                    e1b = jnp.broadcast_to(e18[h][r:r + 1], (pk, LANES)).astype(BF16)
                    g[k] = g[k] + e1b[None] * jnp.where(rk < nbb[None], e2, jnp.zeros_like(e2))
            for k in range(r_share):
                rows = slice((r0 + k - r_lo) * N_KEYS, (r0 + k - r_lo + 1) * N_KEYS)
                hh = ht[rows, cols]
                act = hh * (1.0 + _erf(hh))
                w_ref[rows, cols] = (g[k] * act.astype(BF16).reshape(grp)).reshape(N_KEYS, LANES)


def _peer_kernel(xn_ref, x1_ref, e1_ref, nb_ref, rk_ref, e2_ref, u_ref, vt_ref, gf_ref, y_ref,
                 acc_ref, wa_ref, wb_ref, xs_ref):
    j = pl.program_id(1)
    te = u_ref.shape[0]
    ce = te // 2
    n_sub = te // N_KEYS
    assert n_sub % SUBLANES == 0
    base = pl.multiple_of(j * n_sub, SUBLANES)
    half = n_sub // 2
    grp_b = (half // SUBLANES) * SUBLANES
    routing = (e1_ref, nb_ref, rk_ref, e2_ref)

    @pl.when(j == 0)
    def _():
        acc_ref[...] = jnp.zeros_like(acc_ref)
        xs_ref[...] = xn_ref[...]

    xn = xs_ref[...]
    u = u_ref[...]
    hta = _dot(u[0:ce], xn)
    htb = _dot(u[ce:te], xn)
    _peer_gates(0, half, base, hta, wa_ref, *routing)
    _peer_gates(half - grp_b, n_sub - grp_b, base + grp_b, htb, wb_ref, *routing)
    vt = vt_ref[...]
    acc_ref[...] += _dot(vt[:, 0:ce], wa_ref[...]) + _dot(vt[:, ce:te], wb_ref[...])

    @pl.when(j == pl.num_programs(1) - 1)
    def _():
        x = x1_ref[...] + acc_ref[...].T
        y_ref[...] = x * lax.rsqrt(jnp.mean(x * x, axis=-1, keepdims=True) + EPS) * gf_ref[...]


def _peer(xn, x1, e1, nb, rk, e2, u, vt, gf, tn, te=2 * SUBLANES * N_KEYS):
    n = x1.shape[0]
    n_exp = u.shape[0]
    rspec = pl.BlockSpec((P_HEADS, N_KEYS, tn), lambda i, j: (0, 0, i))
    pspec = pl.BlockSpec((P_HEADS, N_KEYS // 2, tn), lambda i, j: (0, 0, i))
    return pl.pallas_call(
        _peer_kernel,
        grid=(n // tn, n_exp // te),
        in_specs=[
            pl.BlockSpec((D_MODEL, tn), lambda i, j: (0, i)),
            pl.BlockSpec((tn, D_MODEL), lambda i, j: (i, 0)),
            rspec, rspec, pspec, rspec,
            pl.BlockSpec((te, D_MODEL), lambda i, j: (j, 0)),
            pl.BlockSpec((D_MODEL, te), lambda i, j: (0, j)),
            pl.BlockSpec((1, D_MODEL), lambda i, j: (0, 0)),
        ],
        out_specs=pl.BlockSpec((tn, D_MODEL), lambda i, j: (i, 0)),
        out_shape=jax.ShapeDtypeStruct((n, D_MODEL), F32),
        scratch_shapes=[pltpu.VMEM((D_MODEL, tn), F32), pltpu.VMEM((te // 2, tn), BF16),
                        pltpu.VMEM((te // 2, tn), BF16), pltpu.VMEM((D_MODEL, tn), BF16)],
        compiler_params=_cparams(("parallel", "arbitrary")),
    )(xn, x1, e1, nb, rk, e2, u, vt, gf)


def _tile(n, pref):
    return pref if n % pref == 0 else n


def _layer(x, pos, s_ret, s_c, s_n, s_m, s_conv, wts):
    b, t, _ = x.shape
    n = b * t
    x2d = x.reshape(n, D_MODEL)
    proj, ifc, ifr = _proj(x2d, wts["norm1_g"], wts["w_main"], wts["wif_col"], _tile(n, 1024))
    proj3 = proj.reshape(b, t, PROJ_MAIN)
    ifc3 = ifc.reshape(b, t, LANES)
    ifr3 = ifr.reshape(SUBLANES, b, t).transpose(1, 0, 2)
    y_r, ret_new = _retention(proj3, s_ret, pos, wts["ret_norm_g"])
    h_m, c_new, n_new, m_new = _mlstm(proj3, ifc3, ifr3, s_conv, s_c, s_n, s_m, wts["b_i"], wts["b_f"],
                                      wts["conv_w"], wts["conv_b"], wts["w_mq"], wts["w_mk"], wts["mlstm_norm_g"])
    conv_new = proj3[:, t - (CONV_W - 1):, 3 * D_MODEL:4 * D_MODEL]
    x1, xn2, xc2 = _merge(x2d, y_r.reshape(n, D_MODEL), h_m.reshape(n, D_MODEL), proj,
                     wts["w_br"], wts["w_bm"], wts["w_out"], wts["norm2_g"], _tile(n, 512))
    tn = _tile(n, 512)
    e1, nb, rk, e2 = _route(xn2, wts["w_pq_t"], wts["k1"], wts["k2"], tn)
    y = _peer(xc2, x1, e1, nb, rk, e2, wts["u"], wts["vt"], wts["norm_f_g"], tn)
    return (y.reshape(b, t, D_MODEL), ret_new, c_new, n_new.reshape(b, H_M, DH_M), m_new[:, :, 0, 0], conv_new)


def kernel(x_prompt, x_sample, state_ret, state_mlstm_C, state_mlstm_n, state_mlstm_m, state_conv, norm1_g, w_in, b_i, b_f, conv_w, conv_b, w_mq, w_mk, ret_norm_g, mlstm_norm_g, w_br, w_bm, w_out, norm2_g, w_pq, sub_keys1, sub_keys2, expert_u, expert_v, norm_f_g):
    assert w_in.shape[0] == 1, "single-layer step"
    bp, tp, _ = x_prompt.shape
    ts = x_sample.shape[1]
    qk, vr, wm = H_R * DK_R, H_R * DV_R, H_M * DH_M
    w = w_in[0]
    o_if = 2 * qk + 2 * vr + 3 * wm
    w_main = jnp.concatenate([w[:, :o_if], w[:, o_if + 2 * H_M:]], axis=1).astype(BF16)
    w_if = w[:, o_if:o_if + 2 * H_M]
    wts = dict(
        norm1_g=norm1_g[0].reshape(1, D_MODEL),
        w_main=w_main,
        wif_col=jnp.zeros((D_MODEL, LANES), F32).at[:, :2 * H_M].set(w_if),
        b_i=b_i[0], b_f=b_f[0], conv_w=conv_w[0], conv_b=conv_b[0], w_mq=w_mq[0], w_mk=w_mk[0],
        ret_norm_g=ret_norm_g[0], mlstm_norm_g=mlstm_norm_g[0],
        w_br=w_br[0].astype(BF16), w_bm=w_bm[0].astype(BF16), w_out=w_out[0].astype(BF16),
        norm2_g=norm2_g[0].reshape(1, D_MODEL),
        w_pq_t=w_pq[0].T.astype(BF16),
        k1=sub_keys1[0].astype(BF16), k2=sub_keys2[0].astype(BF16),
        u=expert_u[0].astype(BF16), vt=expert_v[0].T.astype(BF16),
        norm_f_g=norm_f_g.reshape(1, D_MODEL),
    )
    pos_p = jnp.arange(tp, dtype=F32)
    pos_s = PAST_LEN + jnp.arange(ts, dtype=F32)
    zp = lambda *s: jnp.zeros((bp,) + s, F32)
    yp, r0, c0, n0, m0, v0 = _layer(x_prompt, pos_p, zp(H_R, DK_R, DV_R), zp(H_M, DH_M, DH_M), zp(H_M, DH_M),
                                    zp(H_M), zp(CONV_W - 1, wm), wts)
    ys, r1, c1, n1, m1, v1 = _layer(x_sample, pos_s, state_ret[0], state_mlstm_C[0], state_mlstm_n[0],
                                    state_mlstm_m[0], state_conv[0], wts)
    return (yp, ys, r0[None], c0[None], n0[None], m0[None], v0[None],
            r1[None], c1[None], n1[None], m1[None], v1[None])
```

```python
import math

import jax
import jax.numpy as jnp
from jax import lax
from jax.experimental import pallas as pl
from jax.experimental.pallas import tpu as pltpu

F32 = jnp.float32
BF16 = jnp.bfloat16
HIGHEST = lax.Precision.HIGHEST

D_MODEL = 1024
H_R, DK_R, DV_R = 4, 128, 256
H_M, DH_M = 4, 256
CONV_W = 4
N_KEYS = 128
P_HEADS = 8
P_HALF = 128
P_TOPK = 16
CHUNK = 128
PAST_LEN = 16384
ROPE_BASE = 10000.0
EPS = 1e-6
LANES = 128
SUBLANES = 8
VMEM_LIMIT = 56 * 1024 * 1024
PROJ_MAIN = 8192
NEG_INF = float("-inf")
GELU_C = 2.0 ** -0.5


def _cparams(sem):
    return pltpu.CompilerParams(dimension_semantics=sem, vmem_limit_bytes=VMEM_LIMIT)


def _dot(a, b):
    return jnp.dot(a, b, preferred_element_type=F32)


def _dot_nt(a, b):
    return lax.dot_general(a, b, (((1,), (1,)), ((), ())), preferred_element_type=F32)


def _dot_tn(a, b):
    return lax.dot_general(a, b, (((0,), (0,)), ((), ())), preferred_element_type=F32)


def _sigmoid(x):
    return 1.0 / (1.0 + jnp.exp(-x))


def _log_sigmoid(x):
    return jnp.minimum(x, 0.0) - jnp.log(1.0 + jnp.exp(-jnp.abs(x)))


def _head_norm(x, g):
    mu = jnp.mean(x, axis=-1, keepdims=True)
    xc = x - mu
    var = jnp.mean(xc * xc, axis=-1, keepdims=True)
    return xc * lax.rsqrt(var + EPS) * g


def _proj_kernel(x_ref, g_ref, w_ref, wifc_ref, o_ref, oc_ref, or_ref, xn_ref):
    @pl.when(pl.program_id(1) == 0)
    def _():
        x = x_ref[...]
        xn = x * lax.rsqrt(jnp.mean(x * x, axis=-1, keepdims=True) + EPS) * g_ref[...]
        xh = xn.astype(BF16)
        xn_ref[...] = xh
        xl = (xn - xh.astype(F32)).astype(BF16)
        x3 = jnp.concatenate([xh, xl, xh], axis=1)
        oc = _dot(x3, wifc_ref[...])
        oc_ref[...] = oc
        or_ref[...] = oc.T[0:SUBLANES, :]

    o_ref[...] = _dot(xn_ref[...], w_ref[...])


def _split3(w, axis):
    hi = w.astype(BF16)
    lo = (w - hi.astype(F32)).astype(BF16)
    return jnp.concatenate([hi, hi, lo], axis=axis)


def _proj(x2d, g, w_main, wif_col, tm, tn=1024):
    n = x2d.shape[0]
    wif_col = _split3(wif_col, 0)
    return pl.pallas_call(
        _proj_kernel,
        grid=(n // tm, PROJ_MAIN // tn),
        in_specs=[
            pl.BlockSpec((tm, D_MODEL), lambda i, j: (i, 0)),
            pl.BlockSpec((1, D_MODEL), lambda i, j: (0, 0)),
            pl.BlockSpec((D_MODEL, tn), lambda i, j: (0, j)),
            pl.BlockSpec((3 * D_MODEL, LANES), lambda i, j: (0, 0)),
        ],
        out_specs=[
            pl.BlockSpec((tm, tn), lambda i, j: (i, j)),
            pl.BlockSpec((tm, LANES), lambda i, j: (i, 0)),
            pl.BlockSpec((SUBLANES, tm), lambda i, j: (0, i)),
        ],
        out_shape=[
            jax.ShapeDtypeStruct((n, PROJ_MAIN), F32),
            jax.ShapeDtypeStruct((n, LANES), F32),
            jax.ShapeDtypeStruct((SUBLANES, n), F32),
        ],
        scratch_shapes=[pltpu.VMEM((tm, D_MODEL), BF16)],
        compiler_params=_cparams(("parallel", "arbitrary")),
    )(x2d, g, w_main, wif_col)


def _batch_block(b, nc, most, most_chunked):
    if nc > 1:
        most = most_chunked
    return max(nb for nb in (1, 2, 4, 8) if nb <= most and b % nb == 0)


def _ret_kernel(cdec_ref, q_ref, k_ref, v_ref, g_ref, cq_ref, sq_ref, ck_ref, sk_ref,
                dmat_ref, qdec_ref, kdec_ref, ng_ref, s0_ref, y_ref, s_ref):
    @pl.when(pl.program_id(1) == 0)
    def _():
        s_ref[...] = s0_ref[...]

    cq, sq, ck, sk = cq_ref[...], sq_ref[...], ck_ref[...], sk_ref[...]
    units = [(bb, h) for bb in range(q_ref.shape[0]) for h in range(H_R)]
    ks = lambda h: slice(h * DK_R, (h + 1) * DK_R)
    vs = lambda h: slice(h * DV_R, (h + 1) * DV_R)
    qr, kr, v, sc, cross = {}, {}, {}, {}, {}
    for u in units:
        bb, h = u
        q, k = q_ref[bb, :, ks(h)], k_ref[bb, :, ks(h)]
        qr[u] = (q * cq + pltpu.roll(q, DK_R // 2, 1) * sq).astype(BF16)
        kr[u] = k * ck + pltpu.roll(k, DK_R // 2, 1) * sk
        v[u] = v_ref[bb, :, vs(h)].astype(BF16)
    for u in units:
        bb, h = u
        sc[u] = _dot_nt(qr[u], kr[u].astype(BF16)) * dmat_ref[h]
        cross[u] = _dot(qr[u], s_ref[bb, h].astype(BF16)) * qdec_ref[h]
    for u in units:
        bb, h = u
        s_ref[bb, h] = cdec_ref[h] * s_ref[bb, h] + _dot_tn((kr[u] * kdec_ref[h]).astype(BF16), v[u])
    for u in units:
        bb, h = u
        y = _head_norm(_dot(sc[u].astype(BF16), v[u]) + cross[u], ng_ref[h])
        gate = g_ref[bb, :, vs(h)]
        y_ref[bb, :, vs(h)] = y * (gate * _sigmoid(gate))


def _retention(proj3, s0, pos, ret_norm_g):
    b, t, _ = proj3.shape
    l = CHUNK if t % CHUNK == 0 else t
    nc = t // l
    half = DK_R // 2
    freqs = jnp.exp(-math.log(ROPE_BASE) * jnp.arange(half, dtype=F32) / half)
    ang = pos[:, None] * freqs[None, :]
    cos, sin = jnp.cos(ang), jnp.sin(ang)
    cos2 = jnp.concatenate([cos, cos], axis=1)
    sin2 = jnp.concatenate([-sin, sin], axis=1)
    scale = DK_R ** -0.5
    log_gamma = jnp.log1p(-jnp.exp2(-5.0 - jnp.arange(H_R, dtype=F32)))
    idx = jnp.arange(l, dtype=F32)
    diff = idx[:, None] - idx[None, :]
    causal = diff >= 0
    dmat = jnp.where(causal[None], jnp.exp(log_gamma[:, None, None] * jnp.where(causal, diff, 0.0)[None]), 0.0)
    qdec = jnp.exp(log_gamma[:, None] * (idx[None, :] + 1.0))[:, :, None]
    kdec = jnp.exp(log_gamma[:, None] * (l - 1.0 - idx[None, :]))[:, :, None]
    cdec = jnp.exp(log_gamma * l)
    tab = pl.BlockSpec((l, DK_R), lambda i, c: (c, 0))
    nb = _batch_block(b, nc, 8, 2)
    return pl.pallas_call(
        _ret_kernel,
        grid=(b // nb, nc),
        in_specs=[
            pl.BlockSpec(memory_space=pltpu.SMEM),
            pl.BlockSpec((nb, l, H_R * DK_R), lambda i, c: (i, c, 0)),
            pl.BlockSpec((nb, l, H_R * DK_R), lambda i, c: (i, c, 1)),
            pl.BlockSpec((nb, l, H_R * DV_R), lambda i, c: (i, c, 1)),
            pl.BlockSpec((nb, l, H_R * DV_R), lambda i, c: (i, c, 2)),
            tab, tab, tab, tab,
            pl.BlockSpec((H_R, l, l), lambda i, c: (0, 0, 0)),
            pl.BlockSpec((H_R, l, 1), lambda i, c: (0, 0, 0)),
            pl.BlockSpec((H_R, l, 1), lambda i, c: (0, 0, 0)),
            pl.BlockSpec((H_R, 1, DV_R), lambda i, c: (0, 0, 0)),
            pl.BlockSpec((nb, H_R, DK_R, DV_R), lambda i, c: (i, 0, 0, 0)),
        ],
        out_specs=[
            pl.BlockSpec((nb, l, H_R * DV_R), lambda i, c: (i, c, 0)),
            pl.BlockSpec((nb, H_R, DK_R, DV_R), lambda i, c: (i, 0, 0, 0)),
        ],
        out_shape=[
            jax.ShapeDtypeStruct((b, t, H_R * DV_R), F32),
            jax.ShapeDtypeStruct((b, H_R, DK_R, DV_R), F32),
        ],
        compiler_params=_cparams(("parallel", "arbitrary")),
    )(cdec, proj3, proj3, proj3, proj3, cos2 * scale, sin2 * scale, cos2, sin2,
      dmat, qdec, kdec, ret_norm_g.reshape(H_R, 1, DV_R), s0)


def _mlstm_kernel(x_ref, v_ref, o_ref, ifc_ref, ifr_ref, bc_ref, br_ref, cw_ref, cb_ref,
                  wq_ref, wk_ref, ng_ref, conv0_ref, c0_ref, n0_ref, m0_ref,
                  h_ref, c_ref, n_ref, m_ref, xp_ref):
    l = x_ref.shape[1]

    @pl.when(pl.program_id(1) == 0)
    def _():
        xp_ref[:, 0:SUBLANES, :] = conv0_ref[...]
        c_ref[...] = c0_ref[...]
        n_ref[...] = n0_ref[...]
        m_ref[...] = m0_ref[...]

    ti = lax.broadcasted_iota(jnp.int32, (l, l), 0)
    si = lax.broadcasted_iota(jnp.int32, (l, l), 1)
    causal = si <= ti
    tri = causal.astype(F32)
    rows = range(x_ref.shape[0])
    units = [(bb, h) for bb in rows for h in range(H_M)]
    hs = lambda h: slice(h * DH_M, (h + 1) * DH_M)
    scale = DH_M ** -0.5

    xs, pre_c, pre_r, b_c, b_r = {}, {}, {}, {}, {}
    for bb in rows:
        xp_ref[bb, SUBLANES:SUBLANES + l, :] = x_ref[bb]
        xc = cb_ref[...]
        for j in range(CONV_W):
            off = SUBLANES - (CONV_W - 1) + j
            xc = xc + xp_ref[bb, off:off + l, :] * cw_ref[j:j + 1, :]
        xp_ref[bb, 0:SUBLANES, :] = xp_ref[bb, l:l + SUBLANES, :]
        xs[bb] = xc * _sigmoid(xc)
        pre_c[bb] = ifc_ref[bb] + bc_ref[...]
        pre_r[bb] = ifr_ref[bb] + br_ref[...]
        b_c[bb] = jnp.dot(tri, _log_sigmoid(pre_c[bb]), preferred_element_type=F32, precision=HIGHEST)
        b_r[bb] = lax.dot_general(_log_sigmoid(pre_r[bb]), tri, (((1,), (1,)), ((), ())),
                                  preferred_element_type=F32, precision=HIGHEST)

    q, k, kb, qb, v = {}, {}, {}, {}, {}
    for u in units:
        bb, h = u
        xh = xs[bb][:, hs(h)].astype(BF16)
        q[u] = _dot(xh, wq_ref[h]) * scale
        k[u] = _dot(xh, wk_ref[h])
        qb[u], kb[u] = q[u].astype(BF16), k[u].astype(BF16)
        v[u] = v_ref[bb, :, hs(h)].astype(BF16)

    bcol, rcol, m_old, m_t, dw, w_prev = {}, {}, {}, {}, {}, {}
    for u in units:
        bb, h = u
        m_old[u] = m_ref[bb, h][:, 0:1]
        bcol[u] = b_c[bb][:, H_M + h:H_M + h + 1]
        rcol[u] = pre_c[bb][:, h:h + 1] - bcol[u]
        rrow = pre_r[bb][h:h + 1, :] - b_r[bb][H_M + h:H_M + h + 1, :]
        logd = jnp.where(causal, bcol[u] + rrow, NEG_INF)
        m_t[u] = jnp.maximum(bcol[u] + m_old[u], jnp.max(logd, axis=-1, keepdims=True))
        dw[u] = jnp.exp(logd - m_t[u])
        w_prev[u] = jnp.exp(bcol[u] + m_old[u] - m_t[u])

    s, cross = {}, {}
    for u in units:
        bb, h = u
        s[u] = _dot_nt(qb[u], kb[u])
        cross[u] = _dot(qb[u], c_ref[bb, h].astype(BF16))

    for u in units:
        bb, h = u
        n_old = n_ref[bb, h]
        sc = s[u] * dw[u]
        num = _dot(sc.astype(BF16), v[u]) + w_prev[u] * cross[u]
        den = jnp.sum(sc, axis=-1, keepdims=True) + w_prev[u] * jnp.sum(q[u] * n_old, axis=-1, keepdims=True)
        hh = num / jnp.maximum(jnp.abs(den), jnp.exp(-m_t[u]))
        h_ref[bb, :, hs(h)] = _head_norm(hh, ng_ref[h]) * _sigmoid(o_ref[bb, :, hs(h)])

    for u in units:
        bb, h = u
        m_new = m_t[u][l - 1:l, :]
        b_last = bcol[u][l - 1:l, :]
        wk = jnp.exp(b_last + rcol[u] - m_new)
        dec = jnp.exp(b_last + m_old[u] - m_new)
        kw = k[u] * wk
        c_ref[bb, h] = dec * c_ref[bb, h] + _dot_tn(kw.astype(BF16), v[u])
        n_ref[bb, h] = dec * n_ref[bb, h] + jnp.sum(kw, axis=0, keepdims=True)
        m_ref[bb, h] = jnp.broadcast_to(m_new, (1, LANES))


def _mlstm(proj3, ifc3, ifr3, conv0, c0, n0, m0, b_i, b_f, conv_w, conv_b, w_mq, w_mk, norm_g):
    b, t, _ = proj3.shape
    l = CHUNK if t % CHUNK == 0 else t
    nc = t // l
    w = H_M * DH_M
    bias = jnp.concatenate([b_i, b_f]).astype(F32)
    bias_c = jnp.zeros((1, LANES), F32).at[0, :2 * H_M].set(bias)
    bias_r = bias.reshape(2 * H_M, 1)
    conv0p = jnp.concatenate([jnp.zeros((b, SUBLANES - (CONV_W - 1), w), F32), conv0], axis=1)
    const2 = lambda i, c: (0, 0)
    const3 = lambda i, c: (0, 0, 0)
    state4 = lambda i, c: (i, 0, 0, 0)
    nb = _batch_block(b, nc, 8, 1)
    return pl.pallas_call(
        _mlstm_kernel,
        grid=(b // nb, nc),
        in_specs=[
            pl.BlockSpec((nb, l, w), lambda i, c: (i, c, 3)),
            pl.BlockSpec((nb, l, w), lambda i, c: (i, c, 4)),
            pl.BlockSpec((nb, l, w), lambda i, c: (i, c, 5)),
            pl.BlockSpec((nb, l, LANES), lambda i, c: (i, c, 0)),
            pl.BlockSpec((nb, SUBLANES, l), lambda i, c: (i, 0, c)),
            pl.BlockSpec((1, LANES), const2),
            pl.BlockSpec((2 * H_M, 1), const2),
            pl.BlockSpec((CONV_W, w), const2),
            pl.BlockSpec((1, w), const2),
            pl.BlockSpec((H_M, DH_M, DH_M), const3),
            pl.BlockSpec((H_M, DH_M, DH_M), const3),
            pl.BlockSpec((H_M, 1, DH_M), const3),
            pl.BlockSpec((nb, SUBLANES, w), lambda i, c: (i, 0, 0)),
            pl.BlockSpec((nb, H_M, DH_M, DH_M), state4),
            pl.BlockSpec((nb, H_M, 1, DH_M), state4),
            pl.BlockSpec((nb, H_M, 1, LANES), state4),
        ],
        out_specs=[
            pl.BlockSpec((nb, l, w), lambda i, c: (i, c, 0)),
            pl.BlockSpec((nb, H_M, DH_M, DH_M), state4),
            pl.BlockSpec((nb, H_M, 1, DH_M), state4),
            pl.BlockSpec((nb, H_M, 1, LANES), state4),
        ],
        out_shape=[
            jax.ShapeDtypeStruct((b, t, w), F32),
            jax.ShapeDtypeStruct((b, H_M, DH_M, DH_M), F32),
            jax.ShapeDtypeStruct((b, H_M, 1, DH_M), F32),
            jax.ShapeDtypeStruct((b, H_M, 1, LANES), F32),
        ],
        scratch_shapes=[pltpu.VMEM((nb, l + SUBLANES, w), F32)],
        compiler_params=_cparams(("parallel", "arbitrary")),
    )(proj3, proj3, proj3, ifc3, ifr3, bias_c, bias_r, conv_w, conv_b.reshape(1, w),
      w_mq.astype(BF16), w_mk.astype(BF16), norm_g.reshape(H_M, 1, DH_M), conv0p,
      c0, n0.reshape(b, H_M, 1, DH_M), jnp.broadcast_to(m0[:, :, None, None], (b, H_M, 1, LANES)))


def _merge_kernel(x_ref, yr_ref, hm_ref, gr_ref, gm_ref, wbr_ref, wbm_ref, wo_ref, g2_ref, x1_ref, xn_ref, xc_ref):
    br = _dot(yr_ref[...].astype(BF16), wbr_ref[...])
    bm = _dot(hm_ref[...].astype(BF16), wbm_ref[...])
    merged = _sigmoid(gr_ref[...]) * br + _sigmoid(gm_ref[...]) * bm
    x1 = x_ref[...] + _dot(merged.astype(BF16), wo_ref[...])
    x1_ref[...] = x1
    xn = x1 * lax.rsqrt(jnp.mean(x1 * x1, axis=-1, keepdims=True) + EPS) * g2_ref[...]
    xt = xn.T
    xn_ref[...] = xt.astype(BF16)
    xc_ref[...] = (xt * GELU_C).astype(BF16)


def _merge(x2d, yr, hm, proj, w_br, w_bm, w_out, g2, tm):
    n = x2d.shape[0]
    row = lambda i: (i, 0)
    const = lambda i: (0, 0)
    wspec = pl.BlockSpec((D_MODEL, D_MODEL), const)
    return pl.pallas_call(
        _merge_kernel,
        grid=(n // tm,),
        in_specs=[
            pl.BlockSpec((tm, D_MODEL), row),
            pl.BlockSpec((tm, D_MODEL), row),
            pl.BlockSpec((tm, D_MODEL), row),
            pl.BlockSpec((tm, D_MODEL), lambda i: (i, 6)),
            pl.BlockSpec((tm, D_MODEL), lambda i: (i, 7)),
            wspec, wspec, wspec,
            pl.BlockSpec((1, D_MODEL), const),
        ],
        out_specs=[pl.BlockSpec((tm, D_MODEL), row), pl.BlockSpec((D_MODEL, tm), lambda i: (0, i)),
                   pl.BlockSpec((D_MODEL, tm), lambda i: (0, i))],
        out_shape=[jax.ShapeDtypeStruct((n, D_MODEL), F32), jax.ShapeDtypeStruct((D_MODEL, n), BF16),
                   jax.ShapeDtypeStruct((D_MODEL, n), BF16)],
        compiler_params=_cparams(("parallel",)),
    )(x2d, yr, hm, proj, proj, w_br, w_bm, w_out, g2)


def _top16(s):
    rows = lax.broadcasted_iota(jnp.int32, (P_TOPK, s.shape[1]), 0)
    top = jnp.zeros((P_TOPK, s.shape[1]), F32)
    rank = jnp.full(s.shape, float(P_TOPK), F32)
    for a in range(P_TOPK):
        m = jnp.max(s, axis=0, keepdims=True)
        hit = s == m
        top = jnp.where(rows == a, m, top)
        rank = jnp.where(hit, float(a), rank)
        s = jnp.where(hit, NEG_INF, s)
    return top, rank


def _bf16_pair(x):
    b = pltpu.bitcast(x.astype(BF16).astype(F32), jnp.uint32)
    return pltpu.bitcast(b | (b >> 16), jnp.int32)


def _route_kernel(xn_ref, wq_ref, k1_ref, k2_ref, e1_ref, nb_ref, rk_ref, e2_ref):
    tn = xn_ref.shape[1]
    qt = _dot(wq_ref[...], xn_ref[...])
    for h in range(P_HEADS):
        base = h * 2 * P_HALF
        s1f = _dot(k1_ref[...], qt[base:base + P_HALF].astype(BF16))
        s2f = _dot(k2_ref[...], qt[base + P_HALF:base + 2 * P_HALF].astype(BF16))
        for c in range(tn // LANES):
            cols = slice(c * LANES, (c + 1) * LANES)
            s1, s2 = s1f[:, cols], s2f[:, cols]
            v1, _ = _top16(s1)
            v2, rank2 = _top16(s2)
            cand = [v1 + v2[0:1]]
            cand += [v1[0:SUBLANES] + v2[b:b + 1] for b in range(1, SUBLANES)]
            cand += [v2[SUBLANES:P_TOPK] + v1[0:1]]
            cand = jnp.concatenate(cand, axis=0)
            cmax = v1[0:1] + v2[0:1]
            z = jnp.zeros_like(cmax)
            tau = cmax
            for _ in range(P_TOPK):
                tau = jnp.max(cand, axis=0, keepdims=True)
                z = z + jnp.exp(tau - cmax)
                cand = jnp.where(cand == tau, NEG_INF, cand)
            nb = jnp.zeros_like(s1)
            for b in range(P_TOPK):
                nb = jnp.where(s1 + v2[b:b + 1] >= tau, float(b + 1), nb)
            e1_ref[h, :, cols] = jnp.exp(s1 - v1[0:1])
            e2_ref[h, :, cols] = jnp.exp(s2 - v2[0:1]) * (GELU_C / z)
            nb_ref[h, :, cols] = _bf16_pair(nb)
            rk_ref[h, :, cols] = pltpu.bitcast(rank2.astype(BF16), jnp.int32)


def _route(xn, wq_t, k1, k2, tn):
    n = xn.shape[1]
    const = lambda i: (0, 0)
    ospec = pl.BlockSpec((P_HEADS, N_KEYS, tn), lambda i: (0, 0, i))
    pspec = pl.BlockSpec((P_HEADS, N_KEYS // 2, tn), lambda i: (0, 0, i))
    full = (P_HEADS, N_KEYS, n)
    return pl.pallas_call(
        _route_kernel,
        grid=(n // tn,),
        in_specs=[
            pl.BlockSpec((D_MODEL, tn), lambda i: (0, i)),
            pl.BlockSpec((P_HEADS * 2 * P_HALF, D_MODEL), const),
            pl.BlockSpec((N_KEYS, P_HALF), const),
            pl.BlockSpec((N_KEYS, P_HALF), const),
        ],
        out_specs=[ospec, ospec, pspec, ospec],
        out_shape=[jax.ShapeDtypeStruct(full, F32), jax.ShapeDtypeStruct(full, jnp.int32),
                   jax.ShapeDtypeStruct((P_HEADS, N_KEYS // 2, n), jnp.int32), jax.ShapeDtypeStruct(full, F32)],
        compiler_params=_cparams(("parallel",)),
    )(xn, wq_t, k1, k2)


def _erf(x):
    return lax.erf(x)


def _peer_gates(r_lo, r_hi, base, ht, w_ref, e1_ref, nb_ref, rk_ref, e2_ref):
    tn = w_ref.shape[1]
    pk = 2 * SUBLANES
    grp = (N_KEYS // pk, pk, LANES)
    r_share = 2
    for c0 in range(0, tn, LANES):
        cols = slice(c0, c0 + LANES)
        nb8 = [nb_ref[h, pl.ds(base, SUBLANES), cols] for h in range(P_HEADS)]
        e18 = [e1_ref[h, pl.ds(base, SUBLANES), cols] for h in range(P_HEADS)]
        for r0 in range(r_lo, r_hi, r_share):
            g = [jnp.zeros(grp, BF16) for _ in range(r_share)]
            for h in range(P_HEADS):
                rk = pltpu.bitcast(rk_ref[h, :, cols], BF16).reshape(grp)
                e2 = e2_ref[h, :, cols].astype(BF16).reshape(grp)
                for k in range(r_share):
                    r = r0 + k
                    nbb = pltpu.bitcast(jnp.broadcast_to(nb8[h][r:r + 1], (SUBLANES, LANES)), BF16)
                    e1b = jnp.broadcast_to(e18[h][r:r + 1], (pk, LANES)).astype(BF16)
                    g[k] = g[k] + e1b[None] * jnp.where(rk < nbb[None], e2, jnp.zeros_like(e2))
            for k in range(r_share):
                rows = slice((r0 + k - r_lo) * N_KEYS, (r0 + k - r_lo + 1) * N_KEYS)
                hh = ht[rows, cols]
                act = hh * (1.0 + _erf(hh))
                w_ref[rows, cols] = (g[k] * act.astype(BF16).reshape(grp)).reshape(N_KEYS, LANES)


def _peer_kernel(xn_ref, x1_ref, e1_ref, nb_ref, rk_ref, e2_ref, u_ref, vt_ref, gf_ref, y_ref,
                 acc_ref, wa_ref, wb_ref, xs_ref):
    j = pl.program_id(1)
    te = u_ref.shape[0]
    ce = te // 2
    n_sub = te // N_KEYS
    assert n_sub % SUBLANES == 0
    base = pl.multiple_of(j * n_sub, SUBLANES)
    half = n_sub // 2
    grp_b = (half // SUBLANES) * SUBLANES
    routing = (e1_ref, nb_ref, rk_ref, e2_ref)

    @pl.when(j == 0)
    def _():
        acc_ref[...] = jnp.zeros_like(acc_ref)
        xs_ref[...] = xn_ref[...]

    xn = xs_ref[...]
    u = u_ref[...]
    hta = _dot(u[0:ce], xn)
    htb = _dot(u[ce:te], xn)
    _peer_gates(0, half, base, hta, wa_ref, *routing)
    _peer_gates(half - grp_b, n_sub - grp_b, base + grp_b, htb, wb_ref, *routing)
    vt = vt_ref[...]
    acc_ref[...] += _dot(vt[:, 0:ce], wa_ref[...]) + _dot(vt[:, ce:te], wb_ref[...])

    @pl.when(j == pl.num_programs(1) - 1)
    def _():
        x = x1_ref[...] + acc_ref[...].T
        y_ref[...] = x * lax.rsqrt(jnp.mean(x * x, axis=-1, keepdims=True) + EPS) * gf_ref[...]


def _peer(xn, x1, e1, nb, rk, e2, u, vt, gf, tn, te=2 * SUBLANES * N_KEYS):
    n = x1.shape[0]
    n_exp = u.shape[0]
    rspec = pl.BlockSpec((P_HEADS, N_KEYS, tn), lambda i, j: (0, 0, i))
    pspec = pl.BlockSpec((P_HEADS, N_KEYS // 2, tn), lambda i, j: (0, 0, i))
    return pl.pallas_call(
        _peer_kernel,
        grid=(n // tn, n_exp // te),
        in_specs=[
            pl.BlockSpec((D_MODEL, tn), lambda i, j: (0, i)),
            pl.BlockSpec((tn, D_MODEL), lambda i, j: (i, 0)),
            rspec, rspec, pspec, rspec,
            pl.BlockSpec((te, D_MODEL), lambda i, j: (j, 0)),
            pl.BlockSpec((D_MODEL, te), lambda i, j: (0, j)),
            pl.BlockSpec((1, D_MODEL), lambda i, j: (0, 0)),
        ],
        out_specs=pl.BlockSpec((tn, D_MODEL), lambda i, j: (i, 0)),
        out_shape=jax.ShapeDtypeStruct((n, D_MODEL), F32),
        scratch_shapes=[pltpu.VMEM((D_MODEL, tn), F32), pltpu.VMEM((te // 2, tn), BF16),
                        pltpu.VMEM((te // 2, tn), BF16), pltpu.VMEM((D_MODEL, tn), BF16)],
        compiler_params=_cparams(("parallel", "arbitrary")),
    )(xn, x1, e1, nb, rk, e2, u, vt, gf)


def _tile(n, pref):
    return pref if n % pref == 0 else n


def _layer(x, pos, s_ret, s_c, s_n, s_m, s_conv, wts):
    b, t, _ = x.shape
    n = b * t
    x2d = x.reshape(n, D_MODEL)
    proj, ifc, ifr = _proj(x2d, wts["norm1_g"], wts["w_main"], wts["wif_col"], _tile(n, 1024))
    proj3 = proj.reshape(b, t, PROJ_MAIN)
    ifc3 = ifc.reshape(b, t, LANES)
    ifr3 = ifr.reshape(SUBLANES, b, t).transpose(1, 0, 2)
    y_r, ret_new = _retention(proj3, s_ret, pos, wts["ret_norm_g"])
    h_m, c_new, n_new, m_new = _mlstm(proj3, ifc3, ifr3, s_conv, s_c, s_n, s_m, wts["b_i"], wts["b_f"],
                                      wts["conv_w"], wts["conv_b"], wts["w_mq"], wts["w_mk"], wts["mlstm_norm_g"])
    conv_new = proj3[:, t - (CONV_W - 1):, 3 * D_MODEL:4 * D_MODEL]
    x1, xn2, xc2 = _merge(x2d, y_r.reshape(n, D_MODEL), h_m.reshape(n, D_MODEL), proj,
                     wts["w_br"], wts["w_bm"], wts["w_out"], wts["norm2_g"], _tile(n, 512))
    tn = _tile(n, 512)
    e1, nb, rk, e2 = _route(xn2, wts["w_pq_t"], wts["k1"], wts["k2"], tn)
    y = _peer(xc2, x1, e1, nb, rk, e2, wts["u"], wts["vt"], wts["norm_f_g"], tn)
    return (y.reshape(b, t, D_MODEL), ret_new, c_new, n_new.reshape(b, H_M, DH_M), m_new[:, :, 0, 0], conv_new)


def kernel(x_prompt, x_sample, state_ret, state_mlstm_C, state_mlstm_n, state_mlstm_m, state_conv, norm1_g, w_in, b_i, b_f, conv_w, conv_b, w_mq, w_mk, ret_norm_g, mlstm_norm_g, w_br, w_bm, w_out, norm2_g, w_pq, sub_keys1, sub_keys2, expert_u, expert_v, norm_f_g):
    assert w_in.shape[0] == 1, "single-layer step"
    bp, tp, _ = x_prompt.shape
    ts = x_sample.shape[1]
    qk, vr, wm = H_R * DK_R, H_R * DV_R, H_M * DH_M
    w = w_in[0]
    o_if = 2 * qk + 2 * vr + 3 * wm
    w_main = jnp.concatenate([w[:, :o_if], w[:, o_if + 2 * H_M:]], axis=1).astype(BF16)
    w_if = w[:, o_if:o_if + 2 * H_M]
    wts = dict(
        norm1_g=norm1_g[0].reshape(1, D_MODEL),
        w_main=w_main,
        wif_col=jnp.zeros((D_MODEL, LANES), F32).at[:, :2 * H_M].set(w_if),
        b_i=b_i[0], b_f=b_f[0], conv_w=conv_w[0], conv_b=conv_b[0], w_mq=w_mq[0], w_mk=w_mk[0],
        ret_norm_g=ret_norm_g[0], mlstm_norm_g=mlstm_norm_g[0],
        w_br=w_br[0].astype(BF16), w_bm=w_bm[0].astype(BF16), w_out=w_out[0].astype(BF16),
        norm2_g=norm2_g[0].reshape(1, D_MODEL),
        w_pq_t=w_pq[0].T.astype(BF16),
        k1=sub_keys1[0].astype(BF16), k2=sub_keys2[0].astype(BF16),
        u=expert_u[0].astype(BF16), vt=expert_v[0].T.astype(BF16),
        norm_f_g=norm_f_g.reshape(1, D_MODEL),
    )
    pos_p = jnp.arange(tp, dtype=F32)
    pos_s = PAST_LEN + jnp.arange(ts, dtype=F32)
    zp = lambda *s: jnp.zeros((bp,) + s, F32)
    yp, r0, c0, n0, m0, v0 = _layer(x_prompt, pos_p, zp(H_R, DK_R, DV_R), zp(H_M, DH_M, DH_M), zp(H_M, DH_M),
                                    zp(H_M), zp(CONV_W - 1, wm), wts)
    ys, r1, c1, n1, m1, v1 = _layer(x_sample, pos_s, state_ret[0], state_mlstm_C[0], state_mlstm_n[0],
                                    state_mlstm_m[0], state_conv[0], wts)
    return (yp, ys, r0[None], c0[None], n0[None], m0[None], v0[None],
            r1[None], c1[None], n1[None], m1[None], v1[None])
```

```python
import math

import jax
import jax.numpy as jnp
from jax import lax
from jax.experimental import pallas as pl
from jax.experimental.pallas import tpu as pltpu

F32 = jnp.float32
BF16 = jnp.bfloat16
HIGHEST = lax.Precision.HIGHEST

D_MODEL = 1024
H_R, DK_R, DV_R = 4, 128, 256
H_M, DH_M = 4, 256
CONV_W = 4
N_KEYS = 128
P_HEADS = 8
P_HALF = 128
P_TOPK = 16
CHUNK = 128
PAST_LEN = 16384
ROPE_BASE = 10000.0
EPS = 1e-6
LANES = 128
SUBLANES = 8
VMEM_LIMIT = 56 * 1024 * 1024
PROJ_MAIN = 8192
NEG_INF = float("-inf")
GELU_C = 2.0 ** -0.5


def _cparams(sem):
    return pltpu.CompilerParams(dimension_semantics=sem, vmem_limit_bytes=VMEM_LIMIT)


def _dot(a, b):
    return jnp.dot(a, b, preferred_element_type=F32)


def _dot_nt(a, b):
    return lax.dot_general(a, b, (((1,), (1,)), ((), ())), preferred_element_type=F32)


def _dot_tn(a, b):
    return lax.dot_general(a, b, (((0,), (0,)), ((), ())), preferred_element_type=F32)


def _sigmoid(x):
    return 1.0 / (1.0 + jnp.exp(-x))


def _log_sigmoid(x):
    return jnp.minimum(x, 0.0) - jnp.log(1.0 + jnp.exp(-jnp.abs(x)))


def _head_norm(x, g):
    mu = jnp.mean(x, axis=-1, keepdims=True)
    xc = x - mu
    var = jnp.mean(xc * xc, axis=-1, keepdims=True)
    return xc * lax.rsqrt(var + EPS) * g


def _proj_kernel(x_ref, g_ref, w_ref, wifc_ref, o_ref, oc_ref, or_ref, xn_ref):
    @pl.when(pl.program_id(1) == 0)
    def _():
        x = x_ref[...]
        xn = x * lax.rsqrt(jnp.mean(x * x, axis=-1, keepdims=True) + EPS) * g_ref[...]
        xh = xn.astype(BF16)
        xn_ref[...] = xh
        xl = (xn - xh.astype(F32)).astype(BF16)
        x3 = jnp.concatenate([xh, xl, xh], axis=1)
        oc = _dot(x3, wifc_ref[...])
        oc_ref[...] = oc
        or_ref[...] = oc.T[0:SUBLANES, :]

    o_ref[...] = _dot(xn_ref[...], w_ref[...])


def _split3(w, axis):
    hi = w.astype(BF16)
    lo = (w - hi.astype(F32)).astype(BF16)
    return jnp.concatenate([hi, hi, lo], axis=axis)


def _proj(x2d, g, w_main, wif_col, tm, tn=1024):
    n = x2d.shape[0]
    wif_col = _split3(wif_col, 0)
    return pl.pallas_call(
        _proj_kernel,
        grid=(n // tm, PROJ_MAIN // tn),
        in_specs=[
            pl.BlockSpec((tm, D_MODEL), lambda i, j: (i, 0)),
            pl.BlockSpec((1, D_MODEL), lambda i, j: (0, 0)),
            pl.BlockSpec((D_MODEL, tn), lambda i, j: (0, j)),
            pl.BlockSpec((3 * D_MODEL, LANES), lambda i, j: (0, 0)),
        ],
        out_specs=[
            pl.BlockSpec((tm, tn), lambda i, j: (i, j)),
            pl.BlockSpec((tm, LANES), lambda i, j: (i, 0)),
            pl.BlockSpec((SUBLANES, tm), lambda i, j: (0, i)),
        ],
        out_shape=[
            jax.ShapeDtypeStruct((n, PROJ_MAIN), F32),
            jax.ShapeDtypeStruct((n, LANES), F32),
            jax.ShapeDtypeStruct((SUBLANES, n), F32),
        ],
        scratch_shapes=[pltpu.VMEM((tm, D_MODEL), BF16)],
        compiler_params=_cparams(("parallel", "arbitrary")),
    )(x2d, g, w_main, wif_col)


def _batch_block(b, nc, most, most_chunked):
    if nc > 1:
        most = most_chunked
    return max(nb for nb in (1, 2, 4, 8) if nb <= most and b % nb == 0)


def _ret_kernel(cdec_ref, q_ref, k_ref, v_ref, g_ref, cq_ref, sq_ref, ck_ref, sk_ref,
                dmat_ref, qdec_ref, kdec_ref, ng_ref, s0_ref, y_ref, s_ref):
    @pl.when(pl.program_id(1) == 0)
    def _():
        s_ref[...] = s0_ref[...]

    cq, sq, ck, sk = cq_ref[...], sq_ref[...], ck_ref[...], sk_ref[...]
    units = [(bb, h) for bb in range(q_ref.shape[0]) for h in range(H_R)]
    ks = lambda h: slice(h * DK_R, (h + 1) * DK_R)
    vs = lambda h: slice(h * DV_R, (h + 1) * DV_R)
    qr, kr, v, sc, cross = {}, {}, {}, {}, {}
    for u in units:
        bb, h = u
        q, k = q_ref[bb, :, ks(h)], k_ref[bb, :, ks(h)]
        qr[u] = (q * cq + pltpu.roll(q, DK_R // 2, 1) * sq).astype(BF16)
        kr[u] = k * ck + pltpu.roll(k, DK_R // 2, 1) * sk
        v[u] = v_ref[bb, :, vs(h)].astype(BF16)
    for u in units:
        bb, h = u
        sc[u] = _dot_nt(qr[u], kr[u].astype(BF16)) * dmat_ref[h]
        cross[u] = _dot(qr[u], s_ref[bb, h].astype(BF16)) * qdec_ref[h]
    for u in units:
        bb, h = u
        s_ref[bb, h] = cdec_ref[h] * s_ref[bb, h] + _dot_tn((kr[u] * kdec_ref[h]).astype(BF16), v[u])
    for u in units:
        bb, h = u
        y = _head_norm(_dot(sc[u].astype(BF16), v[u]) + cross[u], ng_ref[h])
        gate = g_ref[bb, :, vs(h)]
        y_ref[bb, :, vs(h)] = y * (gate * _sigmoid(gate))


def _retention(proj3, s0, pos, ret_norm_g):
    b, t, _ = proj3.shape
    l = CHUNK if t % CHUNK == 0 else t
    nc = t // l
    half = DK_R // 2
    freqs = jnp.exp(-math.log(ROPE_BASE) * jnp.arange(half, dtype=F32) / half)
    ang = pos[:, None] * freqs[None, :]
    cos, sin = jnp.cos(ang), jnp.sin(ang)
    cos2 = jnp.concatenate([cos, cos], axis=1)
    sin2 = jnp.concatenate([-sin, sin], axis=1)
    scale = DK_R ** -0.5
    log_gamma = jnp.log1p(-jnp.exp2(-5.0 - jnp.arange(H_R, dtype=F32)))
    idx = jnp.arange(l, dtype=F32)
    diff = idx[:, None] - idx[None, :]
    causal = diff >= 0
    dmat = jnp.where(causal[None], jnp.exp(log_gamma[:, None, None] * jnp.where(causal, diff, 0.0)[None]), 0.0)
    qdec = jnp.exp(log_gamma[:, None] * (idx[None, :] + 1.0))[:, :, None]
    kdec = jnp.exp(log_gamma[:, None] * (l - 1.0 - idx[None, :]))[:, :, None]
    cdec = jnp.exp(log_gamma * l)
    tab = pl.BlockSpec((l, DK_R), lambda i, c: (c, 0))
    nb = _batch_block(b, nc, 8, 2)
    return pl.pallas_call(
        _ret_kernel,
        grid=(b // nb, nc),
        in_specs=[
            pl.BlockSpec(memory_space=pltpu.SMEM),
            pl.BlockSpec((nb, l, H_R * DK_R), lambda i, c: (i, c, 0)),
            pl.BlockSpec((nb, l, H_R * DK_R), lambda i, c: (i, c, 1)),
            pl.BlockSpec((nb, l, H_R * DV_R), lambda i, c: (i, c, 1)),
            pl.BlockSpec((nb, l, H_R * DV_R), lambda i, c: (i, c, 2)),
            tab, tab, tab, tab,
            pl.BlockSpec((H_R, l, l), lambda i, c: (0, 0, 0)),
            pl.BlockSpec((H_R, l, 1), lambda i, c: (0, 0, 0)),
            pl.BlockSpec((H_R, l, 1), lambda i, c: (0, 0, 0)),
            pl.BlockSpec((H_R, 1, DV_R), lambda i, c: (0, 0, 0)),
            pl.BlockSpec((nb, H_R, DK_R, DV_R), lambda i, c: (i, 0, 0, 0)),
        ],
        out_specs=[
            pl.BlockSpec((nb, l, H_R * DV_R), lambda i, c: (i, c, 0)),
            pl.BlockSpec((nb, H_R, DK_R, DV_R), lambda i, c: (i, 0, 0, 0)),
        ],
        out_shape=[
            jax.ShapeDtypeStruct((b, t, H_R * DV_R), F32),
            jax.ShapeDtypeStruct((b, H_R, DK_R, DV_R), F32),
        ],
        compiler_params=_cparams(("parallel", "arbitrary")),
    )(cdec, proj3, proj3, proj3, proj3, cos2 * scale, sin2 * scale, cos2, sin2,
      dmat, qdec, kdec, ret_norm_g.reshape(H_R, 1, DV_R), s0)


def _mlstm_kernel(x_ref, v_ref, o_ref, ifc_ref, ifr_ref, bc_ref, br_ref, cw_ref, cb_ref,
                  wq_ref, wk_ref, ng_ref, conv0_ref, c0_ref, n0_ref, m0_ref,
                  h_ref, c_ref, n_ref, m_ref, xp_ref):
    l = x_ref.shape[1]

    @pl.when(pl.program_id(1) == 0)
    def _():
        xp_ref[:, 0:SUBLANES, :] = conv0_ref[...]
        c_ref[...] = c0_ref[...]
        n_ref[...] = n0_ref[...]
        m_ref[...] = m0_ref[...]

    ti = lax.broadcasted_iota(jnp.int32, (l, l), 0)
    si = lax.broadcasted_iota(jnp.int32, (l, l), 1)
    causal = si <= ti
    tri = causal.astype(F32)
    rows = range(x_ref.shape[0])
    units = [(bb, h) for bb in rows for h in range(H_M)]
    hs = lambda h: slice(h * DH_M, (h + 1) * DH_M)
    scale = DH_M ** -0.5

    xs, pre_c, pre_r, b_c, b_r = {}, {}, {}, {}, {}
    for bb in rows:
        xp_ref[bb, SUBLANES:SUBLANES + l, :] = x_ref[bb]
        xc = cb_ref[...]
        for j in range(CONV_W):
            off = SUBLANES - (CONV_W - 1) + j
            xc = xc + xp_ref[bb, off:off + l, :] * cw_ref[j:j + 1, :]
        xp_ref[bb, 0:SUBLANES, :] = xp_ref[bb, l:l + SUBLANES, :]
        xs[bb] = xc * _sigmoid(xc)
        pre_c[bb] = ifc_ref[bb] + bc_ref[...]
        pre_r[bb] = ifr_ref[bb] + br_ref[...]
        b_c[bb] = jnp.dot(tri, _log_sigmoid(pre_c[bb]), preferred_element_type=F32, precision=HIGHEST)
        b_r[bb] = lax.dot_general(_log_sigmoid(pre_r[bb]), tri, (((1,), (1,)), ((), ())),
                                  preferred_element_type=F32, precision=HIGHEST)

    q, k, kb, qb, v = {}, {}, {}, {}, {}
    for u in units:
        bb, h = u
        xh = xs[bb][:, hs(h)].astype(BF16)
        q[u] = _dot(xh, wq_ref[h]) * scale
        k[u] = _dot(xh, wk_ref[h])
        qb[u], kb[u] = q[u].astype(BF16), k[u].astype(BF16)
        v[u] = v_ref[bb, :, hs(h)].astype(BF16)

    bcol, rcol, m_old, m_t, dw, w_prev = {}, {}, {}, {}, {}, {}
    for u in units:
        bb, h = u
        m_old[u] = m_ref[bb, h][:, 0:1]
        bcol[u] = b_c[bb][:, H_M + h:H_M + h + 1]
        rcol[u] = pre_c[bb][:, h:h + 1] - bcol[u]
        rrow = pre_r[bb][h:h + 1, :] - b_r[bb][H_M + h:H_M + h + 1, :]
        logd = jnp.where(causal, bcol[u] + rrow, NEG_INF)
        m_t[u] = jnp.maximum(bcol[u] + m_old[u], jnp.max(logd, axis=-1, keepdims=True))
        dw[u] = jnp.exp(logd - m_t[u])
        w_prev[u] = jnp.exp(bcol[u] + m_old[u] - m_t[u])

    s, cross = {}, {}
    for u in units:
        bb, h = u
        s[u] = _dot_nt(qb[u], kb[u])
        cross[u] = _dot(qb[u], c_ref[bb, h].astype(BF16))

    for u in units:
        bb, h = u
        n_old = n_ref[bb, h]
        sc = s[u] * dw[u]
        num = _dot(sc.astype(BF16), v[u]) + w_prev[u] * cross[u]
        den = jnp.sum(sc, axis=-1, keepdims=True) + w_prev[u] * jnp.sum(q[u] * n_old, axis=-1, keepdims=True)
        hh = num / jnp.maximum(jnp.abs(den), jnp.exp(-m_t[u]))
        h_ref[bb, :, hs(h)] = _head_norm(hh, ng_ref[h]) * _sigmoid(o_ref[bb, :, hs(h)])

    for u in units:
        bb, h = u
        m_new = m_t[u][l - 1:l, :]
        b_last = bcol[u][l - 1:l, :]
        wk = jnp.exp(b_last + rcol[u] - m_new)
        dec = jnp.exp(b_last + m_old[u] - m_new)
        kw = k[u] * wk
        c_ref[bb, h] = dec * c_ref[bb, h] + _dot_tn(kw.astype(BF16), v[u])
        n_ref[bb, h] = dec * n_ref[bb, h] + jnp.sum(kw, axis=0, keepdims=True)
        m_ref[bb, h] = jnp.broadcast_to(m_new, (1, LANES))


def _mlstm(proj3, ifc3, ifr3, conv0, c0, n0, m0, b_i, b_f, conv_w, conv_b, w_mq, w_mk, norm_g):
    b, t, _ = proj3.shape
    l = CHUNK if t % CHUNK == 0 else t
    nc = t // l
    w = H_M * DH_M
    bias = jnp.concatenate([b_i, b_f]).astype(F32)
    bias_c = jnp.zeros((1, LANES), F32).at[0, :2 * H_M].set(bias)
    bias_r = bias.reshape(2 * H_M, 1)
    conv0p = jnp.concatenate([jnp.zeros((b, SUBLANES - (CONV_W - 1), w), F32), conv0], axis=1)
    const2 = lambda i, c: (0, 0)
    const3 = lambda i, c: (0, 0, 0)
    state4 = lambda i, c: (i, 0, 0, 0)
    nb = _batch_block(b, nc, 8, 1)
    return pl.pallas_call(
        _mlstm_kernel,
        grid=(b // nb, nc),
        in_specs=[
            pl.BlockSpec((nb, l, w), lambda i, c: (i, c, 3)),
            pl.BlockSpec((nb, l, w), lambda i, c: (i, c, 4)),
            pl.BlockSpec((nb, l, w), lambda i, c: (i, c, 5)),
            pl.BlockSpec((nb, l, LANES), lambda i, c: (i, c, 0)),
            pl.BlockSpec((nb, SUBLANES, l), lambda i, c: (i, 0, c)),
            pl.BlockSpec((1, LANES), const2),
            pl.BlockSpec((2 * H_M, 1), const2),
            pl.BlockSpec((CONV_W, w), const2),
            pl.BlockSpec((1, w), const2),
            pl.BlockSpec((H_M, DH_M, DH_M), const3),
            pl.BlockSpec((H_M, DH_M, DH_M), const3),
            pl.BlockSpec((H_M, 1, DH_M), const3),
            pl.BlockSpec((nb, SUBLANES, w), lambda i, c: (i, 0, 0)),
            pl.BlockSpec((nb, H_M, DH_M, DH_M), state4),
            pl.BlockSpec((nb, H_M, 1, DH_M), state4),
            pl.BlockSpec((nb, H_M, 1, LANES), state4),
        ],
        out_specs=[
            pl.BlockSpec((nb, l, w), lambda i, c: (i, c, 0)),
            pl.BlockSpec((nb, H_M, DH_M, DH_M), state4),
            pl.BlockSpec((nb, H_M, 1, DH_M), state4),
            pl.BlockSpec((nb, H_M, 1, LANES), state4),
        ],
        out_shape=[
            jax.ShapeDtypeStruct((b, t, w), F32),
            jax.ShapeDtypeStruct((b, H_M, DH_M, DH_M), F32),
            jax.ShapeDtypeStruct((b, H_M, 1, DH_M), F32),
            jax.ShapeDtypeStruct((b, H_M, 1, LANES), F32),
        ],
        scratch_shapes=[pltpu.VMEM((nb, l + SUBLANES, w), F32)],
        compiler_params=_cparams(("parallel", "arbitrary")),
    )(proj3, proj3, proj3, ifc3, ifr3, bias_c, bias_r, conv_w, conv_b.reshape(1, w),
      w_mq.astype(BF16), w_mk.astype(BF16), norm_g.reshape(H_M, 1, DH_M), conv0p,
      c0, n0.reshape(b, H_M, 1, DH_M), jnp.broadcast_to(m0[:, :, None, None], (b, H_M, 1, LANES)))


def _merge_kernel(x_ref, yr_ref, hm_ref, gr_ref, gm_ref, wbr_ref, wbm_ref, wo_ref, g2_ref, x1_ref, xn_ref, xc_ref):
    br = _dot(yr_ref[...].astype(BF16), wbr_ref[...])
    bm = _dot(hm_ref[...].astype(BF16), wbm_ref[...])
    merged = _sigmoid(gr_ref[...]) * br + _sigmoid(gm_ref[...]) * bm
    x1 = x_ref[...] + _dot(merged.astype(BF16), wo_ref[...])
    x1_ref[...] = x1
    xn = x1 * lax.rsqrt(jnp.mean(x1 * x1, axis=-1, keepdims=True) + EPS) * g2_ref[...]
    xt = xn.T
    xn_ref[...] = xt.astype(BF16)
    xc_ref[...] = (xt * GELU_C).astype(BF16)


def _merge(x2d, yr, hm, proj, w_br, w_bm, w_out, g2, tm):
    n = x2d.shape[0]
    row = lambda i: (i, 0)
    const = lambda i: (0, 0)
    wspec = pl.BlockSpec((D_MODEL, D_MODEL), const)
    return pl.pallas_call(
        _merge_kernel,
        grid=(n // tm,),
        in_specs=[
            pl.BlockSpec((tm, D_MODEL), row),
            pl.BlockSpec((tm, D_MODEL), row),
            pl.BlockSpec((tm, D_MODEL), row),
            pl.BlockSpec((tm, D_MODEL), lambda i: (i, 6)),
            pl.BlockSpec((tm, D_MODEL), lambda i: (i, 7)),
            wspec, wspec, wspec,
            pl.BlockSpec((1, D_MODEL), const),
        ],
        out_specs=[pl.BlockSpec((tm, D_MODEL), row), pl.BlockSpec((D_MODEL, tm), lambda i: (0, i)),
                   pl.BlockSpec((D_MODEL, tm), lambda i: (0, i))],
        out_shape=[jax.ShapeDtypeStruct((n, D_MODEL), F32), jax.ShapeDtypeStruct((D_MODEL, n), BF16),
                   jax.ShapeDtypeStruct((D_MODEL, n), BF16)],
        compiler_params=_cparams(("parallel",)),
    )(x2d, yr, hm, proj, proj, w_br, w_bm, w_out, g2)


def _top16(s):
    rows = lax.broadcasted_iota(jnp.int32, (P_TOPK, s.shape[1]), 0)
    top = jnp.zeros((P_TOPK, s.shape[1]), F32)
    rank = jnp.full(s.shape, float(P_TOPK), F32)
    for a in range(P_TOPK):
        m = jnp.max(s, axis=0, keepdims=True)
        hit = s == m
        top = jnp.where(rows == a, m, top)
        rank = jnp.where(hit, float(a), rank)
        s = jnp.where(hit, NEG_INF, s)
    return top, rank


def _bf16_pair(x):
    b = pltpu.bitcast(x.astype(BF16).astype(F32), jnp.uint32)
    return pltpu.bitcast(b | (b >> 16), jnp.int32)


def _route_kernel(xn_ref, wq_ref, k1_ref, k2_ref, e1_ref, nb_ref, rk_ref, e2_ref):
    tn = xn_ref.shape[1]
    qt = _dot(wq_ref[...], xn_ref[...])
    for h in range(P_HEADS):
        base = h * 2 * P_HALF
        s1f = _dot(k1_ref[...], qt[base:base + P_HALF].astype(BF16))
        s2f = _dot(k2_ref[...], qt[base + P_HALF:base + 2 * P_HALF].astype(BF16))
        for c in range(tn // LANES):
            cols = slice(c * LANES, (c + 1) * LANES)
            s1, s2 = s1f[:, cols], s2f[:, cols]
            v1, _ = _top16(s1)
            v2, rank2 = _top16(s2)
            cand = [v1 + v2[0:1]]
            cand += [v1[0:SUBLANES] + v2[b:b + 1] for b in range(1, SUBLANES)]
            cand += [v2[SUBLANES:P_TOPK] + v1[0:1]]
            cand = jnp.concatenate(cand, axis=0)
            cmax = v1[0:1] + v2[0:1]
            z = jnp.zeros_like(cmax)
            tau = cmax
            for _ in range(P_TOPK):
                tau = jnp.max(cand, axis=0, keepdims=True)
                z = z + jnp.exp(tau - cmax)
                cand = jnp.where(cand == tau, NEG_INF, cand)
            nb = jnp.zeros_like(s1)
            for b in range(SUBLANES):
                nb = jnp.where(s1 + v2[b:b + 1] >= tau, float(b + 1), nb)
            top = v1[0:1]
            nb_top = jnp.zeros_like(top)
            for b in range(SUBLANES, P_TOPK):
                nb_top = jnp.where(top + v2[b:b + 1] >= tau, float(b + 1), nb_top)
            nb = jnp.where(s1 == top, jnp.maximum(nb, nb_top), nb)
            e1_ref[h, :, cols] = jnp.exp(s1 - v1[0:1])
            e2_ref[h, :, cols] = jnp.exp(s2 - v2[0:1]) * (GELU_C / z)
            nb_ref[h, :, cols] = _bf16_pair(nb)
            rk_ref[h, :, cols] = pltpu.bitcast(rank2.astype(BF16), jnp.int32)


def _route(xn, wq_t, k1, k2, tn):
    n = xn.shape[1]
    const = lambda i: (0, 0)
    ospec = pl.BlockSpec((P_HEADS, N_KEYS, tn), lambda i: (0, 0, i))
    pspec = pl.BlockSpec((P_HEADS, N_KEYS // 2, tn), lambda i: (0, 0, i))
    full = (P_HEADS, N_KEYS, n)
    return pl.pallas_call(
        _route_kernel,
        grid=(n // tn,),
        in_specs=[
            pl.BlockSpec((D_MODEL, tn), lambda i: (0, i)),
            pl.BlockSpec((P_HEADS * 2 * P_HALF, D_MODEL), const),
            pl.BlockSpec((N_KEYS, P_HALF), const),
            pl.BlockSpec((N_KEYS, P_HALF), const),
        ],
        out_specs=[ospec, ospec, pspec, ospec],
        out_shape=[jax.ShapeDtypeStruct(full, F32), jax.ShapeDtypeStruct(full, jnp.int32),
                   jax.ShapeDtypeStruct((P_HEADS, N_KEYS // 2, n), jnp.int32), jax.ShapeDtypeStruct(full, F32)],
        compiler_params=_cparams(("parallel",)),
    )(xn, wq_t, k1, k2)


def _erf(x):
    return lax.erf(x)


def _peer_gates(r_lo, r_hi, base, ht, w_ref, e1_ref, nb_ref, rk_ref, e2_ref):
    tn = w_ref.shape[1]
    pk = 2 * SUBLANES
    grp = (N_KEYS // pk, pk, LANES)
    r_share = 8
    for c0 in range(0, tn, LANES):
        cols = slice(c0, c0 + LANES)
        nb8 = [nb_ref[h, pl.ds(base, SUBLANES), cols] for h in range(P_HEADS)]
        e18 = [e1_ref[h, pl.ds(base, SUBLANES), cols] for h in range(P_HEADS)]
        for r0 in range(r_lo, r_hi, r_share):
            g = [jnp.zeros(grp, BF16) for _ in range(r_share)]
            for h in range(P_HEADS):
                rk = pltpu.bitcast(rk_ref[h, :, cols], BF16).reshape(grp)
                e2 = e2_ref[h, :, cols].astype(BF16).reshape(grp)
                for k in range(r_share):
                    r = r0 + k
                    nbb = pltpu.bitcast(jnp.broadcast_to(nb8[h][r:r + 1], (SUBLANES, LANES)), BF16)
                    e1b = jnp.broadcast_to(e18[h][r:r + 1], (pk, LANES)).astype(BF16)
                    g[k] = g[k] + e1b[None] * jnp.where(rk < nbb[None], e2, jnp.zeros_like(e2))
            for k in range(r_share):
                rows = slice((r0 + k - r_lo) * N_KEYS, (r0 + k - r_lo + 1) * N_KEYS)
                hh = ht[rows, cols]
                act = hh * (1.0 + _erf(hh))
                w_ref[rows, cols] = (g[k] * act.astype(BF16).reshape(grp)).reshape(N_KEYS, LANES)


def _peer_kernel(xn_ref, x1_ref, e1_ref, nb_ref, rk_ref, e2_ref, u_ref, vt_ref, gf_ref, y_ref,
                 acc_ref, wa_ref, wb_ref, xs_ref):
    j = pl.program_id(1)
    te = u_ref.shape[0]
    ce = te // 2
    n_sub = te // N_KEYS
    assert n_sub % SUBLANES == 0
    base = pl.multiple_of(j * n_sub, SUBLANES)
    half = n_sub // 2
    grp_b = (half // SUBLANES) * SUBLANES
    routing = (e1_ref, nb_ref, rk_ref, e2_ref)

    @pl.when(j == 0)
    def _():
        acc_ref[...] = jnp.zeros_like(acc_ref)
        xs_ref[...] = xn_ref[...]

    xn = xs_ref[...]
    u = u_ref[...]
    hta = _dot(u[0:ce], xn)
    htb = _dot(u[ce:te], xn)
    _peer_gates(0, half, base, hta, wa_ref, *routing)
    _peer_gates(half - grp_b, n_sub - grp_b, base + grp_b, htb, wb_ref, *routing)
    vt = vt_ref[...]
    acc_ref[...] += _dot(vt[:, 0:ce], wa_ref[...]) + _dot(vt[:, ce:te], wb_ref[...])

    @pl.when(j == pl.num_programs(1) - 1)
    def _():
        x = x1_ref[...] + acc_ref[...].T
        y_ref[...] = x * lax.rsqrt(jnp.mean(x * x, axis=-1, keepdims=True) + EPS) * gf_ref[...]


def _peer(xn, x1, e1, nb, rk, e2, u, vt, gf, tn, te=2 * SUBLANES * N_KEYS):
    n = x1.shape[0]
    n_exp = u.shape[0]
    rspec = pl.BlockSpec((P_HEADS, N_KEYS, tn), lambda i, j: (0, 0, i))
    pspec = pl.BlockSpec((P_HEADS, N_KEYS // 2, tn), lambda i, j: (0, 0, i))
    return pl.pallas_call(
        _peer_kernel,
        grid=(n // tn, n_exp // te),
        in_specs=[
            pl.BlockSpec((D_MODEL, tn), lambda i, j: (0, i)),
            pl.BlockSpec((tn, D_MODEL), lambda i, j: (i, 0)),
            rspec, rspec, pspec, rspec,
            pl.BlockSpec((te, D_MODEL), lambda i, j: (j, 0)),
            pl.BlockSpec((D_MODEL, te), lambda i, j: (0, j)),
            pl.BlockSpec((1, D_MODEL), lambda i, j: (0, 0)),
        ],
        out_specs=pl.BlockSpec((tn, D_MODEL), lambda i, j: (i, 0)),
        out_shape=jax.ShapeDtypeStruct((n, D_MODEL), F32),
        scratch_shapes=[pltpu.VMEM((D_MODEL, tn), F32), pltpu.VMEM((te // 2, tn), BF16),
                        pltpu.VMEM((te // 2, tn), BF16), pltpu.VMEM((D_MODEL, tn), BF16)],
        compiler_params=_cparams(("parallel", "arbitrary")),
    )(xn, x1, e1, nb, rk, e2, u, vt, gf)


def _tile(n, pref):
    return pref if n % pref == 0 else n


def _layer(x, pos, s_ret, s_c, s_n, s_m, s_conv, wts):
    b, t, _ = x.shape
    n = b * t
    x2d = x.reshape(n, D_MODEL)
    proj, ifc, ifr = _proj(x2d, wts["norm1_g"], wts["w_main"], wts["wif_col"], _tile(n, 1024))
    proj3 = proj.reshape(b, t, PROJ_MAIN)
    ifc3 = ifc.reshape(b, t, LANES)
    ifr3 = ifr.reshape(SUBLANES, b, t).transpose(1, 0, 2)
    y_r, ret_new = _retention(proj3, s_ret, pos, wts["ret_norm_g"])
    h_m, c_new, n_new, m_new = _mlstm(proj3, ifc3, ifr3, s_conv, s_c, s_n, s_m, wts["b_i"], wts["b_f"],
                                      wts["conv_w"], wts["conv_b"], wts["w_mq"], wts["w_mk"], wts["mlstm_norm_g"])
    conv_new = proj3[:, t - (CONV_W - 1):, 3 * D_MODEL:4 * D_MODEL]
    x1, xn2, xc2 = _merge(x2d, y_r.reshape(n, D_MODEL), h_m.reshape(n, D_MODEL), proj,
                     wts["w_br"], wts["w_bm"], wts["w_out"], wts["norm2_g"], _tile(n, 512))
    tn = _tile(n, 512)
    e1, nb, rk, e2 = _route(xn2, wts["w_pq_t"], wts["k1"], wts["k2"], tn)
    y = _peer(xc2, x1, e1, nb, rk, e2, wts["u"], wts["vt"], wts["norm_f_g"], tn)
    return (y.reshape(b, t, D_MODEL), ret_new, c_new, n_new.reshape(b, H_M, DH_M), m_new[:, :, 0, 0], conv_new)


def kernel(x_prompt, x_sample, state_ret, state_mlstm_C, state_mlstm_n, state_mlstm_m, state_conv, norm1_g, w_in, b_i, b_f, conv_w, conv_b, w_mq, w_mk, ret_norm_g, mlstm_norm_g, w_br, w_bm, w_out, norm2_g, w_pq, sub_keys1, sub_keys2, expert_u, expert_v, norm_f_g):
    assert w_in.shape[0] == 1, "single-layer step"
    bp, tp, _ = x_prompt.shape
    ts = x_sample.shape[1]
    qk, vr, wm = H_R * DK_R, H_R * DV_R, H_M * DH_M
    w = w_in[0]
    o_if = 2 * qk + 2 * vr + 3 * wm
    w_main = jnp.concatenate([w[:, :o_if], w[:, o_if + 2 * H_M:]], axis=1).astype(BF16)
    w_if = w[:, o_if:o_if + 2 * H_M]
    wts = dict(
        norm1_g=norm1_g[0].reshape(1, D_MODEL),
        w_main=w_main,
        wif_col=jnp.zeros((D_MODEL, LANES), F32).at[:, :2 * H_M].set(w_if),
        b_i=b_i[0], b_f=b_f[0], conv_w=conv_w[0], conv_b=conv_b[0], w_mq=w_mq[0], w_mk=w_mk[0],
        ret_norm_g=ret_norm_g[0], mlstm_norm_g=mlstm_norm_g[0],
        w_br=w_br[0].astype(BF16), w_bm=w_bm[0].astype(BF16), w_out=w_out[0].astype(BF16),
        norm2_g=norm2_g[0].reshape(1, D_MODEL),
        w_pq_t=w_pq[0].T.astype(BF16),
        k1=sub_keys1[0].astype(BF16), k2=sub_keys2[0].astype(BF16),
        u=expert_u[0].astype(BF16), vt=expert_v[0].T.astype(BF16),
        norm_f_g=norm_f_g.reshape(1, D_MODEL),
    )
    pos_p = jnp.arange(tp, dtype=F32)
    pos_s = PAST_LEN + jnp.arange(ts, dtype=F32)
    zp = lambda *s: jnp.zeros((bp,) + s, F32)
    yp, r0, c0, n0, m0, v0 = _layer(x_prompt, pos_p, zp(H_R, DK_R, DV_R), zp(H_M, DH_M, DH_M), zp(H_M, DH_M),
                                    zp(H_M), zp(CONV_W - 1, wm), wts)
    ys, r1, c1, n1, m1, v1 = _layer(x_sample, pos_s, state_ret[0], state_mlstm_C[0], state_mlstm_n[0],
                                    state_mlstm_m[0], state_conv[0], wts)
    return (yp, ys, r0[None], c0[None], n0[None], m0[None], v0[None],
            r1[None], c1[None], n1[None], m1[None], v1[None])
```

```python
import math

import jax
import jax.numpy as jnp
from jax import lax
from jax.experimental import pallas as pl
from jax.experimental.pallas import tpu as pltpu

F32 = jnp.float32
BF16 = jnp.bfloat16
HIGHEST = lax.Precision.HIGHEST

D_MODEL = 1024
H_R, DK_R, DV_R = 4, 128, 256
H_M, DH_M = 4, 256
CONV_W = 4
N_KEYS = 128
P_HEADS = 8
P_HALF = 128
P_TOPK = 16
CHUNK = 128
PAST_LEN = 16384
ROPE_BASE = 10000.0
EPS = 1e-6
LANES = 128
SUBLANES = 8
VMEM_LIMIT = 56 * 1024 * 1024
PROJ_MAIN = 8192
NEG_INF = float("-inf")
GELU_C = 2.0 ** -0.5


def _cparams(sem):
    return pltpu.CompilerParams(dimension_semantics=sem, vmem_limit_bytes=VMEM_LIMIT)


def _dot(a, b):
    return jnp.dot(a, b, preferred_element_type=F32)


def _dot_nt(a, b):
    return lax.dot_general(a, b, (((1,), (1,)), ((), ())), preferred_element_type=F32)


def _dot_tn(a, b):
    return lax.dot_general(a, b, (((0,), (0,)), ((), ())), preferred_element_type=F32)


def _sigmoid(x):
    return 1.0 / (1.0 + jnp.exp(-x))


def _log_sigmoid(x):
    return jnp.minimum(x, 0.0) - jnp.log(1.0 + jnp.exp(-jnp.abs(x)))


def _head_norm(x, g):
    mu = jnp.mean(x, axis=-1, keepdims=True)
    xc = x - mu
    var = jnp.mean(xc * xc, axis=-1, keepdims=True)
    return xc * lax.rsqrt(var + EPS) * g


def _proj_kernel(x_ref, g_ref, w_ref, wifc_ref, o_ref, oc_ref, or_ref, xn_ref):
    @pl.when(pl.program_id(1) == 0)
    def _():
        x = x_ref[...]
        xn = x * lax.rsqrt(jnp.mean(x * x, axis=-1, keepdims=True) + EPS) * g_ref[...]
        xh = xn.astype(BF16)
        xn_ref[...] = xh
        xl = (xn - xh.astype(F32)).astype(BF16)
        x3 = jnp.concatenate([xh, xl, xh], axis=1)
        oc = _dot(x3, wifc_ref[...])
        oc_ref[...] = oc
        or_ref[...] = oc.T[0:SUBLANES, :]

    o_ref[...] = _dot(xn_ref[...], w_ref[...])


def _split3(w, axis):
    hi = w.astype(BF16)
    lo = (w - hi.astype(F32)).astype(BF16)
    return jnp.concatenate([hi, hi, lo], axis=axis)


def _proj(x2d, g, w_main, wif_col, tm, tn=1024):
    n = x2d.shape[0]
    wif_col = _split3(wif_col, 0)
    return pl.pallas_call(
        _proj_kernel,
        grid=(n // tm, PROJ_MAIN // tn),
        in_specs=[
            pl.BlockSpec((tm, D_MODEL), lambda i, j: (i, 0)),
            pl.BlockSpec((1, D_MODEL), lambda i, j: (0, 0)),
            pl.BlockSpec((D_MODEL, tn), lambda i, j: (0, j)),
            pl.BlockSpec((3 * D_MODEL, LANES), lambda i, j: (0, 0)),
        ],
        out_specs=[
            pl.BlockSpec((tm, tn), lambda i, j: (i, j)),
            pl.BlockSpec((tm, LANES), lambda i, j: (i, 0)),
            pl.BlockSpec((SUBLANES, tm), lambda i, j: (0, i)),
        ],
        out_shape=[
            jax.ShapeDtypeStruct((n, PROJ_MAIN), F32),
            jax.ShapeDtypeStruct((n, LANES), F32),
            jax.ShapeDtypeStruct((SUBLANES, n), F32),
        ],
        scratch_shapes=[pltpu.VMEM((tm, D_MODEL), BF16)],
        compiler_params=_cparams(("parallel", "arbitrary")),
    )(x2d, g, w_main, wif_col)


def _batch_block(b, nc, most, most_chunked):
    if nc > 1:
        most = most_chunked
    return max(nb for nb in (1, 2, 4, 8) if nb <= most and b % nb == 0)


def _ret_kernel(cdec_ref, q_ref, k_ref, v_ref, g_ref, cq_ref, sq_ref, ck_ref, sk_ref,
                dmat_ref, qdec_ref, kdec_ref, ng_ref, s0_ref, y_ref, s_ref):
    @pl.when(pl.program_id(1) == 0)
    def _():
        s_ref[...] = s0_ref[...]

    cq, sq, ck, sk = cq_ref[...], sq_ref[...], ck_ref[...], sk_ref[...]
    units = [(bb, h) for bb in range(q_ref.shape[0]) for h in range(H_R)]
    ks = lambda h: slice(h * DK_R, (h + 1) * DK_R)
    vs = lambda h: slice(h * DV_R, (h + 1) * DV_R)
    qr, kr, v, sc, cross = {}, {}, {}, {}, {}
    for u in units:
        bb, h = u
        q, k = q_ref[bb, :, ks(h)], k_ref[bb, :, ks(h)]
        qr[u] = (q * cq + pltpu.roll(q, DK_R // 2, 1) * sq).astype(BF16)
        kr[u] = k * ck + pltpu.roll(k, DK_R // 2, 1) * sk
        v[u] = v_ref[bb, :, vs(h)].astype(BF16)
    for u in units:
        bb, h = u
        sc[u] = _dot_nt(qr[u], kr[u].astype(BF16)) * dmat_ref[h]
        cross[u] = _dot(qr[u], s_ref[bb, h].astype(BF16)) * qdec_ref[h]
    for u in units:
        bb, h = u
        s_ref[bb, h] = cdec_ref[h] * s_ref[bb, h] + _dot_tn((kr[u] * kdec_ref[h]).astype(BF16), v[u])
    for u in units:
        bb, h = u
        y = _head_norm(_dot(sc[u].astype(BF16), v[u]) + cross[u], ng_ref[h])
        gate = g_ref[bb, :, vs(h)]
        y_ref[bb, :, vs(h)] = y * (gate * _sigmoid(gate))


def _retention(proj3, s0, pos, ret_norm_g):
    b, t, _ = proj3.shape
    l = CHUNK if t % CHUNK == 0 else t
    nc = t // l
    half = DK_R // 2
    freqs = jnp.exp(-math.log(ROPE_BASE) * jnp.arange(half, dtype=F32) / half)
    ang = pos[:, None] * freqs[None, :]
    cos, sin = jnp.cos(ang), jnp.sin(ang)
    cos2 = jnp.concatenate([cos, cos], axis=1)
    sin2 = jnp.concatenate([-sin, sin], axis=1)
    scale = DK_R ** -0.5
    log_gamma = jnp.log1p(-jnp.exp2(-5.0 - jnp.arange(H_R, dtype=F32)))
    idx = jnp.arange(l, dtype=F32)
    diff = idx[:, None] - idx[None, :]
    causal = diff >= 0
    dmat = jnp.where(causal[None], jnp.exp(log_gamma[:, None, None] * jnp.where(causal, diff, 0.0)[None]), 0.0)
    qdec = jnp.exp(log_gamma[:, None] * (idx[None, :] + 1.0))[:, :, None]
    kdec = jnp.exp(log_gamma[:, None] * (l - 1.0 - idx[None, :]))[:, :, None]
    cdec = jnp.exp(log_gamma * l)
    tab = pl.BlockSpec((l, DK_R), lambda i, c: (c, 0))
    nb = _batch_block(b, nc, 8, 2)
    return pl.pallas_call(
        _ret_kernel,
        grid=(b // nb, nc),
        in_specs=[
            pl.BlockSpec(memory_space=pltpu.SMEM),
            pl.BlockSpec((nb, l, H_R * DK_R), lambda i, c: (i, c, 0)),
            pl.BlockSpec((nb, l, H_R * DK_R), lambda i, c: (i, c, 1)),
            pl.BlockSpec((nb, l, H_R * DV_R), lambda i, c: (i, c, 1)),
            pl.BlockSpec((nb, l, H_R * DV_R), lambda i, c: (i, c, 2)),
            tab, tab, tab, tab,
            pl.BlockSpec((H_R, l, l), lambda i, c: (0, 0, 0)),
            pl.BlockSpec((H_R, l, 1), lambda i, c: (0, 0, 0)),
            pl.BlockSpec((H_R, l, 1), lambda i, c: (0, 0, 0)),
            pl.BlockSpec((H_R, 1, DV_R), lambda i, c: (0, 0, 0)),
            pl.BlockSpec((nb, H_R, DK_R, DV_R), lambda i, c: (i, 0, 0, 0)),
        ],
        out_specs=[
            pl.BlockSpec((nb, l, H_R * DV_R), lambda i, c: (i, c, 0)),
            pl.BlockSpec((nb, H_R, DK_R, DV_R), lambda i, c: (i, 0, 0, 0)),
        ],
        out_shape=[
            jax.ShapeDtypeStruct((b, t, H_R * DV_R), F32),
            jax.ShapeDtypeStruct((b, H_R, DK_R, DV_R), F32),
        ],
        compiler_params=_cparams(("parallel", "arbitrary")),
    )(cdec, proj3, proj3, proj3, proj3, cos2 * scale, sin2 * scale, cos2, sin2,
      dmat, qdec, kdec, ret_norm_g.reshape(H_R, 1, DV_R), s0)


def _mlstm_kernel(x_ref, v_ref, o_ref, ifc_ref, ifr_ref, bc_ref, br_ref, cw_ref, cb_ref,
                  wq_ref, wk_ref, ng_ref, conv0_ref, c0_ref, n0_ref, m0_ref,
                  h_ref, c_ref, n_ref, m_ref, xp_ref):
    l = x_ref.shape[1]

    @pl.when(pl.program_id(1) == 0)
    def _():
        xp_ref[:, 0:SUBLANES, :] = conv0_ref[...]
        c_ref[...] = c0_ref[...]
        n_ref[...] = n0_ref[...]
        m_ref[...] = m0_ref[...]

    ti = lax.broadcasted_iota(jnp.int32, (l, l), 0)
    si = lax.broadcasted_iota(jnp.int32, (l, l), 1)
    causal = si <= ti
    tri = causal.astype(F32)
    rows = range(x_ref.shape[0])
    units = [(bb, h) for bb in rows for h in range(H_M)]
    hs = lambda h: slice(h * DH_M, (h + 1) * DH_M)
    scale = DH_M ** -0.5

    xs, pre_c, pre_r, b_c, b_r = {}, {}, {}, {}, {}
    for bb in rows:
        xp_ref[bb, SUBLANES:SUBLANES + l, :] = x_ref[bb]
        xc = cb_ref[...]
        for j in range(CONV_W):
            off = SUBLANES - (CONV_W - 1) + j
            xc = xc + xp_ref[bb, off:off + l, :] * cw_ref[j:j + 1, :]
        xp_ref[bb, 0:SUBLANES, :] = xp_ref[bb, l:l + SUBLANES, :]
        xs[bb] = xc * _sigmoid(xc)
        pre_c[bb] = ifc_ref[bb] + bc_ref[...]
        pre_r[bb] = ifr_ref[bb] + br_ref[...]
        b_c[bb] = jnp.dot(tri, _log_sigmoid(pre_c[bb]), preferred_element_type=F32, precision=HIGHEST)
        b_r[bb] = lax.dot_general(_log_sigmoid(pre_r[bb]), tri, (((1,), (1,)), ((), ())),
                                  preferred_element_type=F32, precision=HIGHEST)

    q, k, kb, qb, v = {}, {}, {}, {}, {}
    for u in units:
        bb, h = u
        xh = xs[bb][:, hs(h)].astype(BF16)
        q[u] = _dot(xh, wq_ref[h]) * scale
        k[u] = _dot(xh, wk_ref[h])
        qb[u], kb[u] = q[u].astype(BF16), k[u].astype(BF16)
        v[u] = v_ref[bb, :, hs(h)].astype(BF16)

    bcol, rcol, m_old, m_t, dw, w_prev = {}, {}, {}, {}, {}, {}
    for u in units:
        bb, h = u
        m_old[u] = m_ref[bb, h][:, 0:1]
        bcol[u] = b_c[bb][:, H_M + h:H_M + h + 1]
        rcol[u] = pre_c[bb][:, h:h + 1] - bcol[u]
        rrow = pre_r[bb][h:h + 1, :] - b_r[bb][H_M + h:H_M + h + 1, :]
        logd = jnp.where(causal, bcol[u] + rrow, NEG_INF)
        m_t[u] = jnp.maximum(bcol[u] + m_old[u], jnp.max(logd, axis=-1, keepdims=True))
        dw[u] = jnp.exp(logd - m_t[u])
        w_prev[u] = jnp.exp(bcol[u] + m_old[u] - m_t[u])

    s, cross = {}, {}
    for u in units:
        bb, h = u
        s[u] = _dot_nt(qb[u], kb[u])
        cross[u] = _dot(qb[u], c_ref[bb, h].astype(BF16))

    for u in units:
        bb, h = u
        n_old = n_ref[bb, h]
        sc = s[u] * dw[u]
        num = _dot(sc.astype(BF16), v[u]) + w_prev[u] * cross[u]
        den = jnp.sum(sc, axis=-1, keepdims=True) + w_prev[u] * jnp.sum(q[u] * n_old, axis=-1, keepdims=True)
        hh = num / jnp.maximum(jnp.abs(den), jnp.exp(-m_t[u]))
        h_ref[bb, :, hs(h)] = _head_norm(hh, ng_ref[h]) * _sigmoid(o_ref[bb, :, hs(h)])

    for u in units:
        bb, h = u
        m_new = m_t[u][l - 1:l, :]
        b_last = bcol[u][l - 1:l, :]
        wk = jnp.exp(b_last + rcol[u] - m_new)
        dec = jnp.exp(b_last + m_old[u] - m_new)
        kw = k[u] * wk
        c_ref[bb, h] = dec * c_ref[bb, h] + _dot_tn(kw.astype(BF16), v[u])
        n_ref[bb, h] = dec * n_ref[bb, h] + jnp.sum(kw, axis=0, keepdims=True)
        m_ref[bb, h] = jnp.broadcast_to(m_new, (1, LANES))


def _mlstm(proj3, ifc3, ifr3, conv0, c0, n0, m0, b_i, b_f, conv_w, conv_b, w_mq, w_mk, norm_g):
    b, t, _ = proj3.shape
    l = CHUNK if t % CHUNK == 0 else t
    nc = t // l
    w = H_M * DH_M
    bias = jnp.concatenate([b_i, b_f]).astype(F32)
    bias_c = jnp.zeros((1, LANES), F32).at[0, :2 * H_M].set(bias)
    bias_r = bias.reshape(2 * H_M, 1)
    conv0p = jnp.concatenate([jnp.zeros((b, SUBLANES - (CONV_W - 1), w), F32), conv0], axis=1)
    const2 = lambda i, c: (0, 0)
    const3 = lambda i, c: (0, 0, 0)
    state4 = lambda i, c: (i, 0, 0, 0)
    nb = _batch_block(b, nc, 8, 1)
    return pl.pallas_call(
        _mlstm_kernel,
        grid=(b // nb, nc),
        in_specs=[
            pl.BlockSpec((nb, l, w), lambda i, c: (i, c, 3)),
            pl.BlockSpec((nb, l, w), lambda i, c: (i, c, 4)),
            pl.BlockSpec((nb, l, w), lambda i, c: (i, c, 5)),
            pl.BlockSpec((nb, l, LANES), lambda i, c: (i, c, 0)),
            pl.BlockSpec((nb, SUBLANES, l), lambda i, c: (i, 0, c)),
            pl.BlockSpec((1, LANES), const2),
            pl.BlockSpec((2 * H_M, 1), const2),
            pl.BlockSpec((CONV_W, w), const2),
            pl.BlockSpec((1, w), const2),
            pl.BlockSpec((H_M, DH_M, DH_M), const3),
            pl.BlockSpec((H_M, DH_M, DH_M), const3),
            pl.BlockSpec((H_M, 1, DH_M), const3),
            pl.BlockSpec((nb, SUBLANES, w), lambda i, c: (i, 0, 0)),
            pl.BlockSpec((nb, H_M, DH_M, DH_M), state4),
            pl.BlockSpec((nb, H_M, 1, DH_M), state4),
            pl.BlockSpec((nb, H_M, 1, LANES), state4),
        ],
        out_specs=[
            pl.BlockSpec((nb, l, w), lambda i, c: (i, c, 0)),
            pl.BlockSpec((nb, H_M, DH_M, DH_M), state4),
            pl.BlockSpec((nb, H_M, 1, DH_M), state4),
            pl.BlockSpec((nb, H_M, 1, LANES), state4),
        ],
        out_shape=[
            jax.ShapeDtypeStruct((b, t, w), F32),
            jax.ShapeDtypeStruct((b, H_M, DH_M, DH_M), F32),
            jax.ShapeDtypeStruct((b, H_M, 1, DH_M), F32),
            jax.ShapeDtypeStruct((b, H_M, 1, LANES), F32),
        ],
        scratch_shapes=[pltpu.VMEM((nb, l + SUBLANES, w), F32)],
        compiler_params=_cparams(("parallel", "arbitrary")),
    )(proj3, proj3, proj3, ifc3, ifr3, bias_c, bias_r, conv_w, conv_b.reshape(1, w),
      w_mq.astype(BF16), w_mk.astype(BF16), norm_g.reshape(H_M, 1, DH_M), conv0p,
      c0, n0.reshape(b, H_M, 1, DH_M), jnp.broadcast_to(m0[:, :, None, None], (b, H_M, 1, LANES)))


def _merge_kernel(x_ref, yr_ref, hm_ref, gr_ref, gm_ref, wbr_ref, wbm_ref, wo_ref, g2_ref, x1_ref, xn_ref, xc_ref):
    br = _dot(yr_ref[...].astype(BF16), wbr_ref[...])
    bm = _dot(hm_ref[...].astype(BF16), wbm_ref[...])
    merged = _sigmoid(gr_ref[...]) * br + _sigmoid(gm_ref[...]) * bm
    x1 = x_ref[...] + _dot(merged.astype(BF16), wo_ref[...])
    x1_ref[...] = x1
    xn = x1 * lax.rsqrt(jnp.mean(x1 * x1, axis=-1, keepdims=True) + EPS) * g2_ref[...]
    xt = xn.T
    xn_ref[...] = xt.astype(BF16)
    xc_ref[...] = (xt * GELU_C).astype(BF16)


def _merge(x2d, yr, hm, proj, w_br, w_bm, w_out, g2, tm):
    n = x2d.shape[0]
    row = lambda i: (i, 0)
    const = lambda i: (0, 0)
    wspec = pl.BlockSpec((D_MODEL, D_MODEL), const)
    return pl.pallas_call(
        _merge_kernel,
        grid=(n // tm,),
        in_specs=[
            pl.BlockSpec((tm, D_MODEL), row),
            pl.BlockSpec((tm, D_MODEL), row),
            pl.BlockSpec((tm, D_MODEL), row),
            pl.BlockSpec((tm, D_MODEL), lambda i: (i, 6)),
            pl.BlockSpec((tm, D_MODEL), lambda i: (i, 7)),
            wspec, wspec, wspec,
            pl.BlockSpec((1, D_MODEL), const),
        ],
        out_specs=[pl.BlockSpec((tm, D_MODEL), row), pl.BlockSpec((D_MODEL, tm), lambda i: (0, i)),
                   pl.BlockSpec((D_MODEL, tm), lambda i: (0, i))],
        out_shape=[jax.ShapeDtypeStruct((n, D_MODEL), F32), jax.ShapeDtypeStruct((D_MODEL, n), BF16),
                   jax.ShapeDtypeStruct((D_MODEL, n), BF16)],
        compiler_params=_cparams(("parallel",)),
    )(x2d, yr, hm, proj, proj, w_br, w_bm, w_out, g2)


def _sort16_network():
    pairs = []

    def merge(lo, n, r):
        step = r * 2
        if step < n:
            merge(lo, n, step)
            merge(lo + r, n, step)
            pairs.extend((i, i + r) for i in range(lo + r, lo + n - r, step))
        else:
            pairs.append((lo, lo + r))

    def sort(lo, n):
        if n > 1:
            sort(lo, n // 2)
            sort(lo + n // 2, n // 2)
            merge(lo, n, 1)

    sort(0, P_TOPK)
    return pairs


def _top16(s, want_rank):
    assert s.shape[0] == P_TOPK * SUBLANES
    v = [s[i * SUBLANES:(i + 1) * SUBLANES] for i in range(P_TOPK)]
    for i, j in _sort16_network():
        v[i], v[j] = jnp.maximum(v[i], v[j]), jnp.minimum(v[i], v[j])
    for shift in (4, 2, 1):
        w = [pltpu.roll(x, shift, 0) for x in v]
        v = [jnp.maximum(v[i], w[P_TOPK - 1 - i]) for i in range(P_TOPK)]
        d = P_TOPK // 2
        while d >= 1:
            for i in range(P_TOPK):
                if not i & d:
                    v[i], v[i + d] = jnp.maximum(v[i], v[i + d]), jnp.minimum(v[i], v[i + d])
            d //= 2
    top = jnp.concatenate([x[0:1] for x in v], axis=0)
    if not want_rank:
        return top, None
    rank = jnp.zeros(s.shape, F32)
    for a in range(P_TOPK):
        rank = rank + jnp.where(top[a:a + 1] > s, 1.0, 0.0)
    return top, rank


def _bf16_pair(x):
    b = pltpu.bitcast(x.astype(BF16).astype(F32), jnp.uint32)
    return pltpu.bitcast(b | (b >> 16), jnp.int32)


def _route_kernel(xn_ref, wq_ref, k1_ref, k2_ref, e1_ref, nb_ref, rk_ref, e2_ref):
    tn = xn_ref.shape[1]
    qt = _dot(wq_ref[...], xn_ref[...])
    for h in range(P_HEADS):
        base = h * 2 * P_HALF
        s1f = _dot(k1_ref[...], qt[base:base + P_HALF].astype(BF16))
        s2f = _dot(k2_ref[...], qt[base + P_HALF:base + 2 * P_HALF].astype(BF16))
        for c in range(tn // LANES):
            cols = slice(c * LANES, (c + 1) * LANES)
            s1, s2 = s1f[:, cols], s2f[:, cols]
            v1, _ = _top16(s1, False)
            v2, rank2 = _top16(s2, True)
            cand = [v1 + v2[0:1]]
            cand += [v1[0:SUBLANES] + v2[b:b + 1] for b in range(1, SUBLANES)]
            cand += [v2[SUBLANES:P_TOPK] + v1[0:1]]
            cand = jnp.concatenate(cand, axis=0)
            cmax = v1[0:1] + v2[0:1]
            z = jnp.zeros_like(cmax)
            tau = cmax
            for _ in range(P_TOPK):
                tau = jnp.max(cand, axis=0, keepdims=True)
                z = z + jnp.exp(tau - cmax)
                cand = jnp.where(cand == tau, NEG_INF, cand)
            nb = jnp.zeros_like(s1)
            for b in range(SUBLANES):
                nb = jnp.where(s1 + v2[b:b + 1] >= tau, float(b + 1), nb)
            top = v1[0:1]
            nb_top = jnp.zeros_like(top)
            for b in range(SUBLANES, P_TOPK):
                nb_top = jnp.where(top + v2[b:b + 1] >= tau, float(b + 1), nb_top)
            nb = jnp.where(s1 == top, jnp.maximum(nb, nb_top), nb)
            e1_ref[h, :, cols] = jnp.exp(s1 - v1[0:1])
            e2_ref[h, :, cols] = jnp.exp(s2 - v2[0:1]) * (GELU_C / z)
            nb_ref[h, :, cols] = _bf16_pair(nb)
            rk_ref[h, :, cols] = pltpu.bitcast(rank2.astype(BF16), jnp.int32)


def _route(xn, wq_t, k1, k2, tn):
    n = xn.shape[1]
    const = lambda i: (0, 0)
    ospec = pl.BlockSpec((P_HEADS, N_KEYS, tn), lambda i: (0, 0, i))
    pspec = pl.BlockSpec((P_HEADS, N_KEYS // 2, tn), lambda i: (0, 0, i))
    full = (P_HEADS, N_KEYS, n)
    return pl.pallas_call(
        _route_kernel,
        grid=(n // tn,),
        in_specs=[
            pl.BlockSpec((D_MODEL, tn), lambda i: (0, i)),
            pl.BlockSpec((P_HEADS * 2 * P_HALF, D_MODEL), const),
            pl.BlockSpec((N_KEYS, P_HALF), const),
            pl.BlockSpec((N_KEYS, P_HALF), const),
        ],
        out_specs=[ospec, ospec, pspec, ospec],
        out_shape=[jax.ShapeDtypeStruct(full, F32), jax.ShapeDtypeStruct(full, jnp.int32),
                   jax.ShapeDtypeStruct((P_HEADS, N_KEYS // 2, n), jnp.int32), jax.ShapeDtypeStruct(full, F32)],
        compiler_params=_cparams(("parallel",)),
    )(xn, wq_t, k1, k2)


def _erf(x):
    return lax.erf(x)


def _peer_gates(r_lo, r_hi, base, ht, w_ref, e1_ref, nb_ref, rk_ref, e2_ref):
    tn = w_ref.shape[1]
    pk = 2 * SUBLANES
    grp = (N_KEYS // pk, pk, LANES)
    r_share = 8
    for c0 in range(0, tn, LANES):
        cols = slice(c0, c0 + LANES)
        nb8 = [nb_ref[h, pl.ds(base, SUBLANES), cols] for h in range(P_HEADS)]
        e18 = [e1_ref[h, pl.ds(base, SUBLANES), cols] for h in range(P_HEADS)]
        for r0 in range(r_lo, r_hi, r_share):
            g = [jnp.zeros(grp, BF16) for _ in range(r_share)]
            for h in range(P_HEADS):
                rk = pltpu.bitcast(rk_ref[h, :, cols], BF16).reshape(grp)
                e2 = e2_ref[h, :, cols].astype(BF16).reshape(grp)
                for k in range(r_share):
                    r = r0 + k
                    nbb = pltpu.bitcast(jnp.broadcast_to(nb8[h][r:r + 1], (SUBLANES, LANES)), BF16)
                    e1b = jnp.broadcast_to(e18[h][r:r + 1], (pk, LANES)).astype(BF16)
                    g[k] = g[k] + e1b[None] * jnp.where(rk < nbb[None], e2, jnp.zeros_like(e2))
            for k in range(r_share):
                rows = slice((r0 + k - r_lo) * N_KEYS, (r0 + k - r_lo + 1) * N_KEYS)
                hh = ht[rows, cols]
                act = hh * (1.0 + _erf(hh))
                w_ref[rows, cols] = (g[k] * act.astype(BF16).reshape(grp)).reshape(N_KEYS, LANES)


def _peer_kernel(xn_ref, x1_ref, e1_ref, nb_ref, rk_ref, e2_ref, u_ref, vt_ref, gf_ref, y_ref,
                 acc_ref, wa_ref, wb_ref, xs_ref):
    j = pl.program_id(1)
    te = u_ref.shape[0]
    ce = te // 2
    n_sub = te // N_KEYS
    assert n_sub % SUBLANES == 0
    base = pl.multiple_of(j * n_sub, SUBLANES)
    half = n_sub // 2
    grp_b = (half // SUBLANES) * SUBLANES
    routing = (e1_ref, nb_ref, rk_ref, e2_ref)

    @pl.when(j == 0)
    def _():
        acc_ref[...] = jnp.zeros_like(acc_ref)
        xs_ref[...] = xn_ref[...]

    xn = xs_ref[...]
    u = u_ref[...]
    hta = _dot(u[0:ce], xn)
    htb = _dot(u[ce:te], xn)
    _peer_gates(0, half, base, hta, wa_ref, *routing)
    _peer_gates(half - grp_b, n_sub - grp_b, base + grp_b, htb, wb_ref, *routing)
    vt = vt_ref[...]
    acc_ref[...] += _dot(vt[:, 0:ce], wa_ref[...]) + _dot(vt[:, ce:te], wb_ref[...])

    @pl.when(j == pl.num_programs(1) - 1)
    def _():
        x = x1_ref[...] + acc_ref[...].T
        y_ref[...] = x * lax.rsqrt(jnp.mean(x * x, axis=-1, keepdims=True) + EPS) * gf_ref[...]


def _peer(xn, x1, e1, nb, rk, e2, u, vt, gf, tn, te=2 * SUBLANES * N_KEYS):
    n = x1.shape[0]
    n_exp = u.shape[0]
    rspec = pl.BlockSpec((P_HEADS, N_KEYS, tn), lambda i, j: (0, 0, i))
    pspec = pl.BlockSpec((P_HEADS, N_KEYS // 2, tn), lambda i, j: (0, 0, i))
    return pl.pallas_call(
        _peer_kernel,
        grid=(n // tn, n_exp // te),
        in_specs=[
            pl.BlockSpec((D_MODEL, tn), lambda i, j: (0, i)),
            pl.BlockSpec((tn, D_MODEL), lambda i, j: (i, 0)),
            rspec, rspec, pspec, rspec,
            pl.BlockSpec((te, D_MODEL), lambda i, j: (j, 0)),
            pl.BlockSpec((D_MODEL, te), lambda i, j: (0, j)),
            pl.BlockSpec((1, D_MODEL), lambda i, j: (0, 0)),
        ],
        out_specs=pl.BlockSpec((tn, D_MODEL), lambda i, j: (i, 0)),
        out_shape=jax.ShapeDtypeStruct((n, D_MODEL), F32),
        scratch_shapes=[pltpu.VMEM((D_MODEL, tn), F32), pltpu.VMEM((te // 2, tn), BF16),
                        pltpu.VMEM((te // 2, tn), BF16), pltpu.VMEM((D_MODEL, tn), BF16)],
        compiler_params=_cparams(("parallel", "arbitrary")),
    )(xn, x1, e1, nb, rk, e2, u, vt, gf)


def _tile(n, pref):
    return pref if n % pref == 0 else n


def _layer(x, pos, s_ret, s_c, s_n, s_m, s_conv, wts):
    b, t, _ = x.shape
    n = b * t
    x2d = x.reshape(n, D_MODEL)
    proj, ifc, ifr = _proj(x2d, wts["norm1_g"], wts["w_main"], wts["wif_col"], _tile(n, 1024))
    proj3 = proj.reshape(b, t, PROJ_MAIN)
    ifc3 = ifc.reshape(b, t, LANES)
    ifr3 = ifr.reshape(SUBLANES, b, t).transpose(1, 0, 2)
    y_r, ret_new = _retention(proj3, s_ret, pos, wts["ret_norm_g"])
    h_m, c_new, n_new, m_new = _mlstm(proj3, ifc3, ifr3, s_conv, s_c, s_n, s_m, wts["b_i"], wts["b_f"],
                                      wts["conv_w"], wts["conv_b"], wts["w_mq"], wts["w_mk"], wts["mlstm_norm_g"])
    conv_new = proj3[:, t - (CONV_W - 1):, 3 * D_MODEL:4 * D_MODEL]
    x1, xn2, xc2 = _merge(x2d, y_r.reshape(n, D_MODEL), h_m.reshape(n, D_MODEL), proj,
                     wts["w_br"], wts["w_bm"], wts["w_out"], wts["norm2_g"], _tile(n, 512))
    tn = _tile(n, 512)
    e1, nb, rk, e2 = _route(xn2, wts["w_pq_t"], wts["k1"], wts["k2"], tn)
    y = _peer(xc2, x1, e1, nb, rk, e2, wts["u"], wts["vt"], wts["norm_f_g"], tn)
    return (y.reshape(b, t, D_MODEL), ret_new, c_new, n_new.reshape(b, H_M, DH_M), m_new[:, :, 0, 0], conv_new)


def kernel(x_prompt, x_sample, state_ret, state_mlstm_C, state_mlstm_n, state_mlstm_m, state_conv, norm1_g, w_in, b_i, b_f, conv_w, conv_b, w_mq, w_mk, ret_norm_g, mlstm_norm_g, w_br, w_bm, w_out, norm2_g, w_pq, sub_keys1, sub_keys2, expert_u, expert_v, norm_f_g):
    assert w_in.shape[0] == 1, "single-layer step"
    bp, tp, _ = x_prompt.shape
    ts = x_sample.shape[1]
    qk, vr, wm = H_R * DK_R, H_R * DV_R, H_M * DH_M
    w = w_in[0]
    o_if = 2 * qk + 2 * vr + 3 * wm
    w_main = jnp.concatenate([w[:, :o_if], w[:, o_if + 2 * H_M:]], axis=1).astype(BF16)
    w_if = w[:, o_if:o_if + 2 * H_M]
    wts = dict(
        norm1_g=norm1_g[0].reshape(1, D_MODEL),
        w_main=w_main,
        wif_col=jnp.zeros((D_MODEL, LANES), F32).at[:, :2 * H_M].set(w_if),
        b_i=b_i[0], b_f=b_f[0], conv_w=conv_w[0], conv_b=conv_b[0], w_mq=w_mq[0], w_mk=w_mk[0],
        ret_norm_g=ret_norm_g[0], mlstm_norm_g=mlstm_norm_g[0],
        w_br=w_br[0].astype(BF16), w_bm=w_bm[0].astype(BF16), w_out=w_out[0].astype(BF16),
        norm2_g=norm2_g[0].reshape(1, D_MODEL),
        w_pq_t=w_pq[0].T.astype(BF16),
        k1=sub_keys1[0].astype(BF16), k2=sub_keys2[0].astype(BF16),
        u=expert_u[0].astype(BF16), vt=expert_v[0].T.astype(BF16),
        norm_f_g=norm_f_g.reshape(1, D_MODEL),
    )
    pos_p = jnp.arange(tp, dtype=F32)
    pos_s = PAST_LEN + jnp.arange(ts, dtype=F32)
    zp = lambda *s: jnp.zeros((bp,) + s, F32)
    yp, r0, c0, n0, m0, v0 = _layer(x_prompt, pos_p, zp(H_R, DK_R, DV_R), zp(H_M, DH_M, DH_M), zp(H_M, DH_M),
                                    zp(H_M), zp(CONV_W - 1, wm), wts)
    ys, r1, c1, n1, m1, v1 = _layer(x_sample, pos_s, state_ret[0], state_mlstm_C[0], state_mlstm_n[0],
                                    state_mlstm_m[0], state_conv[0], wts)
    return (yp, ys, r0[None], c0[None], n0[None], m0[None], v0[None],
            r1[None], c1[None], n1[None], m1[None], v1[None])
```

```python
import math

import jax
import jax.numpy as jnp
from jax import lax
from jax.experimental import pallas as pl
from jax.experimental.pallas import tpu as pltpu

F32 = jnp.float32
BF16 = jnp.bfloat16
HIGHEST = lax.Precision.HIGHEST

D_MODEL = 1024
H_R, DK_R, DV_R = 4, 128, 256
H_M, DH_M = 4, 256
CONV_W = 4
N_KEYS = 128
P_HEADS = 8
P_HALF = 128
P_TOPK = 16
CHUNK = 128
PAST_LEN = 16384
ROPE_BASE = 10000.0
EPS = 1e-6
LANES = 128
SUBLANES = 8
VMEM_LIMIT = 56 * 1024 * 1024
PROJ_MAIN = 8192
NEG_INF = float("-inf")
GELU_C = 2.0 ** -0.5


def _cparams(sem):
    return pltpu.CompilerParams(dimension_semantics=sem, vmem_limit_bytes=VMEM_LIMIT)


def _dot(a, b):
    return jnp.dot(a, b, preferred_element_type=F32)


def _dot_nt(a, b):
    return lax.dot_general(a, b, (((1,), (1,)), ((), ())), preferred_element_type=F32)


def _dot_tn(a, b):
    return lax.dot_general(a, b, (((0,), (0,)), ((), ())), preferred_element_type=F32)


def _sigmoid(x):
    return 1.0 / (1.0 + jnp.exp(-x))


def _log_sigmoid(x):
    return jnp.minimum(x, 0.0) - jnp.log(1.0 + jnp.exp(-jnp.abs(x)))


def _head_norm(x, g):
    mu = jnp.mean(x, axis=-1, keepdims=True)
    xc = x - mu
    var = jnp.mean(xc * xc, axis=-1, keepdims=True)
    return xc * lax.rsqrt(var + EPS) * g


def _proj_kernel(x_ref, g_ref, w_ref, wifc_ref, o_ref, oc_ref, or_ref, xn_ref):
    @pl.when(pl.program_id(1) == 0)
    def _():
        x = x_ref[...]
        xn = x * lax.rsqrt(jnp.mean(x * x, axis=-1, keepdims=True) + EPS) * g_ref[...]
        xh = xn.astype(BF16)
        xn_ref[...] = xh
        xl = (xn - xh.astype(F32)).astype(BF16)
        x3 = jnp.concatenate([xh, xl, xh], axis=1)
        oc = _dot(x3, wifc_ref[...])
        oc_ref[...] = oc
        or_ref[...] = oc.T[0:SUBLANES, :]

    o_ref[...] = _dot(xn_ref[...], w_ref[...])


def _split3(w, axis):
    hi = w.astype(BF16)
    lo = (w - hi.astype(F32)).astype(BF16)
    return jnp.concatenate([hi, hi, lo], axis=axis)


def _proj(x2d, g, w_main, wif_col, tm, tn=1024):
    n = x2d.shape[0]
    wif_col = _split3(wif_col, 0)
    return pl.pallas_call(
        _proj_kernel,
        grid=(n // tm, PROJ_MAIN // tn),
        in_specs=[
            pl.BlockSpec((tm, D_MODEL), lambda i, j: (i, 0)),
            pl.BlockSpec((1, D_MODEL), lambda i, j: (0, 0)),
            pl.BlockSpec((D_MODEL, tn), lambda i, j: (0, j)),
            pl.BlockSpec((3 * D_MODEL, LANES), lambda i, j: (0, 0)),
        ],
        out_specs=[
            pl.BlockSpec((tm, tn), lambda i, j: (i, j)),
            pl.BlockSpec((tm, LANES), lambda i, j: (i, 0)),
            pl.BlockSpec((SUBLANES, tm), lambda i, j: (0, i)),
        ],
        out_shape=[
            jax.ShapeDtypeStruct((n, PROJ_MAIN), F32),
            jax.ShapeDtypeStruct((n, LANES), F32),
            jax.ShapeDtypeStruct((SUBLANES, n), F32),
        ],
        scratch_shapes=[pltpu.VMEM((tm, D_MODEL), BF16)],
        compiler_params=_cparams(("parallel", "arbitrary")),
    )(x2d, g, w_main, wif_col)


def _batch_block(b, nc, most, most_chunked):
    if nc > 1:
        most = most_chunked
    return max(nb for nb in (1, 2, 4, 8) if nb <= most and b % nb == 0)


def _ret_kernel(cdec_ref, q_ref, k_ref, v_ref, g_ref, cq_ref, sq_ref, ck_ref, sk_ref,
                dmat_ref, qdec_ref, kdec_ref, ng_ref, s0_ref, y_ref, s_ref):
    @pl.when(pl.program_id(1) == 0)
    def _():
        s_ref[...] = s0_ref[...]

    cq, sq, ck, sk = cq_ref[...], sq_ref[...], ck_ref[...], sk_ref[...]
    units = [(bb, h) for bb in range(q_ref.shape[0]) for h in range(H_R)]
    ks = lambda h: slice(h * DK_R, (h + 1) * DK_R)
    vs = lambda h: slice(h * DV_R, (h + 1) * DV_R)
    qr, kr, v, sc, cross = {}, {}, {}, {}, {}
    for u in units:
        bb, h = u
        q, k = q_ref[bb, :, ks(h)], k_ref[bb, :, ks(h)]
        qr[u] = (q * cq + pltpu.roll(q, DK_R // 2, 1) * sq).astype(BF16)
        kr[u] = k * ck + pltpu.roll(k, DK_R // 2, 1) * sk
        v[u] = v_ref[bb, :, vs(h)].astype(BF16)
    for u in units:
        bb, h = u
        sc[u] = _dot_nt(qr[u], kr[u].astype(BF16)) * dmat_ref[h]
        cross[u] = _dot(qr[u], s_ref[bb, h].astype(BF16)) * qdec_ref[h]
    for u in units:
        bb, h = u
        s_ref[bb, h] = cdec_ref[h] * s_ref[bb, h] + _dot_tn((kr[u] * kdec_ref[h]).astype(BF16), v[u])
    for u in units:
        bb, h = u
        y = _head_norm(_dot(sc[u].astype(BF16), v[u]) + cross[u], ng_ref[h])
        gate = g_ref[bb, :, vs(h)]
        y_ref[bb, :, vs(h)] = y * (gate * _sigmoid(gate))


def _retention(proj3, s0, pos, ret_norm_g):
    b, t, _ = proj3.shape
    l = CHUNK if t % CHUNK == 0 else t
    nc = t // l
    half = DK_R // 2
    freqs = jnp.exp(-math.log(ROPE_BASE) * jnp.arange(half, dtype=F32) / half)
    ang = pos[:, None] * freqs[None, :]
    cos, sin = jnp.cos(ang), jnp.sin(ang)
    cos2 = jnp.concatenate([cos, cos], axis=1)
    sin2 = jnp.concatenate([-sin, sin], axis=1)
    scale = DK_R ** -0.5
    log_gamma = jnp.log1p(-jnp.exp2(-5.0 - jnp.arange(H_R, dtype=F32)))
    idx = jnp.arange(l, dtype=F32)
    diff = idx[:, None] - idx[None, :]
    causal = diff >= 0
    dmat = jnp.where(causal[None], jnp.exp(log_gamma[:, None, None] * jnp.where(causal, diff, 0.0)[None]), 0.0)
    qdec = jnp.exp(log_gamma[:, None] * (idx[None, :] + 1.0))[:, :, None]
    kdec = jnp.exp(log_gamma[:, None] * (l - 1.0 - idx[None, :]))[:, :, None]
    cdec = jnp.exp(log_gamma * l)
    tab = pl.BlockSpec((l, DK_R), lambda i, c: (c, 0))
    nb = _batch_block(b, nc, 8, 2)
    return pl.pallas_call(
        _ret_kernel,
        grid=(b // nb, nc),
        in_specs=[
            pl.BlockSpec(memory_space=pltpu.SMEM),
            pl.BlockSpec((nb, l, H_R * DK_R), lambda i, c: (i, c, 0)),
            pl.BlockSpec((nb, l, H_R * DK_R), lambda i, c: (i, c, 1)),
            pl.BlockSpec((nb, l, H_R * DV_R), lambda i, c: (i, c, 1)),
            pl.BlockSpec((nb, l, H_R * DV_R), lambda i, c: (i, c, 2)),
            tab, tab, tab, tab,
            pl.BlockSpec((H_R, l, l), lambda i, c: (0, 0, 0)),
            pl.BlockSpec((H_R, l, 1), lambda i, c: (0, 0, 0)),
            pl.BlockSpec((H_R, l, 1), lambda i, c: (0, 0, 0)),
            pl.BlockSpec((H_R, 1, DV_R), lambda i, c: (0, 0, 0)),
            pl.BlockSpec((nb, H_R, DK_R, DV_R), lambda i, c: (i, 0, 0, 0)),
        ],
        out_specs=[
            pl.BlockSpec((nb, l, H_R * DV_R), lambda i, c: (i, c, 0)),
            pl.BlockSpec((nb, H_R, DK_R, DV_R), lambda i, c: (i, 0, 0, 0)),
        ],
        out_shape=[
            jax.ShapeDtypeStruct((b, t, H_R * DV_R), F32),
            jax.ShapeDtypeStruct((b, H_R, DK_R, DV_R), F32),
        ],
        compiler_params=_cparams(("parallel", "arbitrary")),
    )(cdec, proj3, proj3, proj3, proj3, cos2 * scale, sin2 * scale, cos2, sin2,
      dmat, qdec, kdec, ret_norm_g.reshape(H_R, 1, DV_R), s0)


def _mlstm_kernel(x_ref, v_ref, o_ref, ifc_ref, ifr_ref, bc_ref, br_ref, cw_ref, cb_ref,
                  wq_ref, wk_ref, ng_ref, conv0_ref, c0_ref, n0_ref, m0_ref,
                  h_ref, c_ref, n_ref, m_ref, xp_ref):
    l = x_ref.shape[1]

    @pl.when(pl.program_id(1) == 0)
    def _():
        xp_ref[:, 0:SUBLANES, :] = conv0_ref[...]
        c_ref[...] = c0_ref[...]
        n_ref[...] = n0_ref[...]
        m_ref[...] = m0_ref[...]

    ti = lax.broadcasted_iota(jnp.int32, (l, l), 0)
    si = lax.broadcasted_iota(jnp.int32, (l, l), 1)
    causal = si <= ti
    tri = causal.astype(F32)
    rows = range(x_ref.shape[0])
    units = [(bb, h) for bb in rows for h in range(H_M)]
    hs = lambda h: slice(h * DH_M, (h + 1) * DH_M)
    scale = DH_M ** -0.5

    xs, pre_c, pre_r, b_c, b_r = {}, {}, {}, {}, {}
    for bb in rows:
        xp_ref[bb, SUBLANES:SUBLANES + l, :] = x_ref[bb]
        xc = cb_ref[...]
        for j in range(CONV_W):
            off = SUBLANES - (CONV_W - 1) + j
            xc = xc + xp_ref[bb, off:off + l, :] * cw_ref[j:j + 1, :]
        xp_ref[bb, 0:SUBLANES, :] = xp_ref[bb, l:l + SUBLANES, :]
        xs[bb] = xc * _sigmoid(xc)
        pre_c[bb] = ifc_ref[bb] + bc_ref[...]
        pre_r[bb] = ifr_ref[bb] + br_ref[...]
        b_c[bb] = jnp.dot(tri, _log_sigmoid(pre_c[bb]), preferred_element_type=F32, precision=HIGHEST)
        b_r[bb] = lax.dot_general(_log_sigmoid(pre_r[bb]), tri, (((1,), (1,)), ((), ())),
                                  preferred_element_type=F32, precision=HIGHEST)

    q, k, kb, qb, v = {}, {}, {}, {}, {}
    for u in units:
        bb, h = u
        xh = xs[bb][:, hs(h)].astype(BF16)
        q[u] = _dot(xh, wq_ref[h]) * scale
        k[u] = _dot(xh, wk_ref[h])
        qb[u], kb[u] = q[u].astype(BF16), k[u].astype(BF16)
        v[u] = v_ref[bb, :, hs(h)].astype(BF16)

    bcol, rcol, m_old, m_t, dw, w_prev = {}, {}, {}, {}, {}, {}
    for u in units:
        bb, h = u
        m_old[u] = m_ref[bb, h][:, 0:1]
        bcol[u] = b_c[bb][:, H_M + h:H_M + h + 1]
        rcol[u] = pre_c[bb][:, h:h + 1] - bcol[u]
        rrow = pre_r[bb][h:h + 1, :] - b_r[bb][H_M + h:H_M + h + 1, :]
        logd = jnp.where(causal, bcol[u] + rrow, NEG_INF)
        m_t[u] = jnp.maximum(bcol[u] + m_old[u], jnp.max(logd, axis=-1, keepdims=True))
        dw[u] = jnp.exp(logd - m_t[u])
        w_prev[u] = jnp.exp(bcol[u] + m_old[u] - m_t[u])

    s, cross = {}, {}
    for u in units:
        bb, h = u
        s[u] = _dot_nt(qb[u], kb[u])
        cross[u] = _dot(qb[u], c_ref[bb, h].astype(BF16))

    for u in units:
        bb, h = u
        n_old = n_ref[bb, h]
        sc = s[u] * dw[u]
        num = _dot(sc.astype(BF16), v[u]) + w_prev[u] * cross[u]
        den = jnp.sum(sc, axis=-1, keepdims=True) + w_prev[u] * jnp.sum(q[u] * n_old, axis=-1, keepdims=True)
        hh = num / jnp.maximum(jnp.abs(den), jnp.exp(-m_t[u]))
        h_ref[bb, :, hs(h)] = _head_norm(hh, ng_ref[h]) * _sigmoid(o_ref[bb, :, hs(h)])

    for u in units:
        bb, h = u
        m_new = m_t[u][l - 1:l, :]
        b_last = bcol[u][l - 1:l, :]
        wk = jnp.exp(b_last + rcol[u] - m_new)
        dec = jnp.exp(b_last + m_old[u] - m_new)
        kw = k[u] * wk
        c_ref[bb, h] = dec * c_ref[bb, h] + _dot_tn(kw.astype(BF16), v[u])
        n_ref[bb, h] = dec * n_ref[bb, h] + jnp.sum(kw, axis=0, keepdims=True)
        m_ref[bb, h] = jnp.broadcast_to(m_new, (1, LANES))


def _mlstm(proj3, ifc3, ifr3, conv0, c0, n0, m0, b_i, b_f, conv_w, conv_b, w_mq, w_mk, norm_g):
    b, t, _ = proj3.shape
    l = CHUNK if t % CHUNK == 0 else t
    nc = t // l
    w = H_M * DH_M
    bias = jnp.concatenate([b_i, b_f]).astype(F32)
    bias_c = jnp.zeros((1, LANES), F32).at[0, :2 * H_M].set(bias)
    bias_r = bias.reshape(2 * H_M, 1)
    conv0p = jnp.concatenate([jnp.zeros((b, SUBLANES - (CONV_W - 1), w), F32), conv0], axis=1)
    const2 = lambda i, c: (0, 0)
    const3 = lambda i, c: (0, 0, 0)
    state4 = lambda i, c: (i, 0, 0, 0)
    nb = _batch_block(b, nc, 8, 1)
    return pl.pallas_call(
        _mlstm_kernel,
        grid=(b // nb, nc),
        in_specs=[
            pl.BlockSpec((nb, l, w), lambda i, c: (i, c, 3)),
            pl.BlockSpec((nb, l, w), lambda i, c: (i, c, 4)),
            pl.BlockSpec((nb, l, w), lambda i, c: (i, c, 5)),
            pl.BlockSpec((nb, l, LANES), lambda i, c: (i, c, 0)),
            pl.BlockSpec((nb, SUBLANES, l), lambda i, c: (i, 0, c)),
            pl.BlockSpec((1, LANES), const2),
            pl.BlockSpec((2 * H_M, 1), const2),
            pl.BlockSpec((CONV_W, w), const2),
            pl.BlockSpec((1, w), const2),
            pl.BlockSpec((H_M, DH_M, DH_M), const3),
            pl.BlockSpec((H_M, DH_M, DH_M), const3),
            pl.BlockSpec((H_M, 1, DH_M), const3),
            pl.BlockSpec((nb, SUBLANES, w), lambda i, c: (i, 0, 0)),
            pl.BlockSpec((nb, H_M, DH_M, DH_M), state4),
            pl.BlockSpec((nb, H_M, 1, DH_M), state4),
            pl.BlockSpec((nb, H_M, 1, LANES), state4),
        ],
        out_specs=[
            pl.BlockSpec((nb, l, w), lambda i, c: (i, c, 0)),
            pl.BlockSpec((nb, H_M, DH_M, DH_M), state4),
            pl.BlockSpec((nb, H_M, 1, DH_M), state4),
            pl.BlockSpec((nb, H_M, 1, LANES), state4),
        ],
        out_shape=[
            jax.ShapeDtypeStruct((b, t, w), F32),
            jax.ShapeDtypeStruct((b, H_M, DH_M, DH_M), F32),
            jax.ShapeDtypeStruct((b, H_M, 1, DH_M), F32),
            jax.ShapeDtypeStruct((b, H_M, 1, LANES), F32),
        ],
        scratch_shapes=[pltpu.VMEM((nb, l + SUBLANES, w), F32)],
        compiler_params=_cparams(("parallel", "arbitrary")),
    )(proj3, proj3, proj3, ifc3, ifr3, bias_c, bias_r, conv_w, conv_b.reshape(1, w),
      w_mq.astype(BF16), w_mk.astype(BF16), norm_g.reshape(H_M, 1, DH_M), conv0p,
      c0, n0.reshape(b, H_M, 1, DH_M), jnp.broadcast_to(m0[:, :, None, None], (b, H_M, 1, LANES)))


def _merge_kernel(x_ref, yr_ref, hm_ref, gr_ref, gm_ref, wbr_ref, wbm_ref, wo_ref, g2_ref, x1_ref, xn_ref, xc_ref):
    br = _dot(yr_ref[...].astype(BF16), wbr_ref[...])
    bm = _dot(hm_ref[...].astype(BF16), wbm_ref[...])
    merged = _sigmoid(gr_ref[...]) * br + _sigmoid(gm_ref[...]) * bm
    x1 = x_ref[...] + _dot(merged.astype(BF16), wo_ref[...])
    x1_ref[...] = x1
    xn = x1 * lax.rsqrt(jnp.mean(x1 * x1, axis=-1, keepdims=True) + EPS) * g2_ref[...]
    xt = xn.T
    xn_ref[...] = xt.astype(BF16)
    xc_ref[...] = (xt * GELU_C).astype(BF16)


def _merge(x2d, yr, hm, proj, w_br, w_bm, w_out, g2, tm):
    n = x2d.shape[0]
    row = lambda i: (i, 0)
    const = lambda i: (0, 0)
    wspec = pl.BlockSpec((D_MODEL, D_MODEL), const)
    return pl.pallas_call(
        _merge_kernel,
        grid=(n // tm,),
        in_specs=[
            pl.BlockSpec((tm, D_MODEL), row),
            pl.BlockSpec((tm, D_MODEL), row),
            pl.BlockSpec((tm, D_MODEL), row),
            pl.BlockSpec((tm, D_MODEL), lambda i: (i, 6)),
            pl.BlockSpec((tm, D_MODEL), lambda i: (i, 7)),
            wspec, wspec, wspec,
            pl.BlockSpec((1, D_MODEL), const),
        ],
        out_specs=[pl.BlockSpec((tm, D_MODEL), row), pl.BlockSpec((D_MODEL, tm), lambda i: (0, i)),
                   pl.BlockSpec((D_MODEL, tm), lambda i: (0, i))],
        out_shape=[jax.ShapeDtypeStruct((n, D_MODEL), F32), jax.ShapeDtypeStruct((D_MODEL, n), BF16),
                   jax.ShapeDtypeStruct((D_MODEL, n), BF16)],
        compiler_params=_cparams(("parallel",)),
    )(x2d, yr, hm, proj, proj, w_br, w_bm, w_out, g2)


def _sort16_network():
    pairs = []

    def merge(lo, n, r):
        step = r * 2
        if step < n:
            merge(lo, n, step)
            merge(lo + r, n, step)
            pairs.extend((i, i + r) for i in range(lo + r, lo + n - r, step))
        else:
            pairs.append((lo, lo + r))

    def sort(lo, n):
        if n > 1:
            sort(lo, n // 2)
            sort(lo + n // 2, n // 2)
            merge(lo, n, 1)

    sort(0, P_TOPK)
    return pairs


def _top16(s, want_rank):
    assert s.shape[0] == P_TOPK * SUBLANES
    v = [s[i * SUBLANES:(i + 1) * SUBLANES] for i in range(P_TOPK)]
    for i, j in _sort16_network():
        v[i], v[j] = jnp.maximum(v[i], v[j]), jnp.minimum(v[i], v[j])
    for shift in (4, 2, 1):
        w = [pltpu.roll(x, shift, 0) for x in v]
        v = [jnp.maximum(v[i], w[P_TOPK - 1 - i]) for i in range(P_TOPK)]
        d = P_TOPK // 2
        while d >= 1:
            for i in range(P_TOPK):
                if not i & d:
                    v[i], v[i + d] = jnp.maximum(v[i], v[i + d]), jnp.minimum(v[i], v[i + d])
            d //= 2
    top = jnp.concatenate([x[0:1] for x in v], axis=0)
    if not want_rank:
        return top, None
    rank = jnp.zeros(s.shape, F32)
    for a in range(P_TOPK):
        rank = jnp.where(top[a:a + 1] > s, float(a + 1), rank)
    return top, rank


def _bf16_pair(x):
    b = pltpu.bitcast(x.astype(BF16).astype(F32), jnp.uint32)
    return pltpu.bitcast(b | (b >> 16), jnp.int32)


def _route_kernel(xn_ref, wq_ref, k1_ref, k2_ref, e1_ref, nb_ref, rk_ref, e2_ref):
    tn = xn_ref.shape[1]
    qt = _dot(wq_ref[...], xn_ref[...])
    for h in range(P_HEADS):
        base = h * 2 * P_HALF
        s1f = _dot(k1_ref[...], qt[base:base + P_HALF].astype(BF16))
        s2f = _dot(k2_ref[...], qt[base + P_HALF:base + 2 * P_HALF].astype(BF16))
        for c in range(tn // LANES):
            cols = slice(c * LANES, (c + 1) * LANES)
            s1, s2 = s1f[:, cols], s2f[:, cols]
            v1, _ = _top16(s1, False)
            v2, rank2 = _top16(s2, True)
            cand = [v1 + v2[0:1]]
            cand += [v1[0:SUBLANES] + v2[b:b + 1] for b in range(1, SUBLANES)]
            cand += [v2[SUBLANES:P_TOPK] + v1[0:1]]
            cand += [jnp.full((N_KEYS - 10 * SUBLANES, LANES), NEG_INF, F32)]
            cand = jnp.concatenate(cand, axis=0)
            ctop, _ = _top16(cand, False)
            tau = ctop[P_TOPK - 1:P_TOPK]
            z = jnp.sum(jnp.exp(ctop - ctop[0:1]), axis=0, keepdims=True)
            nb = jnp.zeros_like(s1)
            for b in range(SUBLANES):
                nb = jnp.where(s1 + v2[b:b + 1] >= tau, float(b + 1), nb)
            top = v1[0:1]
            nb_top = jnp.zeros_like(top)
            for b in range(SUBLANES, P_TOPK):
                nb_top = jnp.where(top + v2[b:b + 1] >= tau, float(b + 1), nb_top)
            nb = jnp.where(s1 == top, jnp.maximum(nb, nb_top), nb)
            e1_ref[h, :, cols] = jnp.exp(s1 - v1[0:1])
            e2_ref[h, :, cols] = jnp.exp(s2 - v2[0:1]) * (GELU_C / z)
            nb_ref[h, :, cols] = _bf16_pair(nb)
            rk_ref[h, :, cols] = pltpu.bitcast(rank2.astype(BF16), jnp.int32)


def _route(xn, wq_t, k1, k2, tn):
    n = xn.shape[1]
    const = lambda i: (0, 0)
    ospec = pl.BlockSpec((P_HEADS, N_KEYS, tn), lambda i: (0, 0, i))
    pspec = pl.BlockSpec((P_HEADS, N_KEYS // 2, tn), lambda i: (0, 0, i))
    full = (P_HEADS, N_KEYS, n)
    return pl.pallas_call(
        _route_kernel,
        grid=(n // tn,),
        in_specs=[
            pl.BlockSpec((D_MODEL, tn), lambda i: (0, i)),
            pl.BlockSpec((P_HEADS * 2 * P_HALF, D_MODEL), const),
            pl.BlockSpec((N_KEYS, P_HALF), const),
            pl.BlockSpec((N_KEYS, P_HALF), const),
        ],
        out_specs=[ospec, ospec, pspec, ospec],
        out_shape=[jax.ShapeDtypeStruct(full, F32), jax.ShapeDtypeStruct(full, jnp.int32),
                   jax.ShapeDtypeStruct((P_HEADS, N_KEYS // 2, n), jnp.int32), jax.ShapeDtypeStruct(full, F32)],
        compiler_params=_cparams(("parallel",)),
    )(xn, wq_t, k1, k2)


def _erf(x):
    return lax.erf(x)


def _peer_gates(r_lo, r_hi, base, ht, w_ref, e1_ref, nb_ref, rk_ref, e2_ref):
    tn = w_ref.shape[1]
    pk = 2 * SUBLANES
    grp = (N_KEYS // pk, pk, LANES)
    r_share = 8
    for c0 in range(0, tn, LANES):
        cols = slice(c0, c0 + LANES)
        nb8 = [nb_ref[h, pl.ds(base, SUBLANES), cols] for h in range(P_HEADS)]
        e18 = [e1_ref[h, pl.ds(base, SUBLANES), cols] for h in range(P_HEADS)]
        for r0 in range(r_lo, r_hi, r_share):
            g = [jnp.zeros(grp, BF16) for _ in range(r_share)]
            for h in range(P_HEADS):
                rk = pltpu.bitcast(rk_ref[h, :, cols], BF16).reshape(grp)
                e2 = e2_ref[h, :, cols].astype(BF16).reshape(grp)
                for k in range(r_share):
                    r = r0 + k
                    nbb = pltpu.bitcast(jnp.broadcast_to(nb8[h][r:r + 1], (SUBLANES, LANES)), BF16)
                    e1b = jnp.broadcast_to(e18[h][r:r + 1], (pk, LANES)).astype(BF16)
                    g[k] = g[k] + e1b[None] * jnp.where(rk < nbb[None], e2, jnp.zeros_like(e2))
            for k in range(r_share):
                rows = slice((r0 + k - r_lo) * N_KEYS, (r0 + k - r_lo + 1) * N_KEYS)
                hh = ht[rows, cols]
                act = hh * (1.0 + _erf(hh))
                w_ref[rows, cols] = (g[k] * act.astype(BF16).reshape(grp)).reshape(N_KEYS, LANES)


def _peer_kernel(xn_ref, x1_ref, e1_ref, nb_ref, rk_ref, e2_ref, u_ref, vt_ref, gf_ref, y_ref,
                 acc_ref, wa_ref, wb_ref, xs_ref):
    j = pl.program_id(1)
    te = u_ref.shape[0]
    ce = te // 2
    n_sub = te // N_KEYS
    assert n_sub % SUBLANES == 0
    base = pl.multiple_of(j * n_sub, SUBLANES)
    half = n_sub // 2
    grp_b = (half // SUBLANES) * SUBLANES
    routing = (e1_ref, nb_ref, rk_ref, e2_ref)

    @pl.when(j == 0)
    def _():
        acc_ref[...] = jnp.zeros_like(acc_ref)
        xs_ref[...] = xn_ref[...]

    xn = xs_ref[...]
    u = u_ref[...]
    hta = _dot(u[0:ce], xn)
    htb = _dot(u[ce:te], xn)
    _peer_gates(0, half, base, hta, wa_ref, *routing)
    _peer_gates(half - grp_b, n_sub - grp_b, base + grp_b, htb, wb_ref, *routing)
    vt = vt_ref[...]
    acc_ref[...] += _dot(vt[:, 0:ce], wa_ref[...]) + _dot(vt[:, ce:te], wb_ref[...])

    @pl.when(j == pl.num_programs(1) - 1)
    def _():
        x = x1_ref[...] + acc_ref[...].T
        y_ref[...] = x * lax.rsqrt(jnp.mean(x * x, axis=-1, keepdims=True) + EPS) * gf_ref[...]


def _peer(xn, x1, e1, nb, rk, e2, u, vt, gf, tn, te=2 * SUBLANES * N_KEYS):
    n = x1.shape[0]
    n_exp = u.shape[0]
    rspec = pl.BlockSpec((P_HEADS, N_KEYS, tn), lambda i, j: (0, 0, i))
    pspec = pl.BlockSpec((P_HEADS, N_KEYS // 2, tn), lambda i, j: (0, 0, i))
    return pl.pallas_call(
        _peer_kernel,
        grid=(n // tn, n_exp // te),
        in_specs=[
            pl.BlockSpec((D_MODEL, tn), lambda i, j: (0, i)),
            pl.BlockSpec((tn, D_MODEL), lambda i, j: (i, 0)),
            rspec, rspec, pspec, rspec,
            pl.BlockSpec((te, D_MODEL), lambda i, j: (j, 0)),
            pl.BlockSpec((D_MODEL, te), lambda i, j: (0, j)),
            pl.BlockSpec((1, D_MODEL), lambda i, j: (0, 0)),
        ],
        out_specs=pl.BlockSpec((tn, D_MODEL), lambda i, j: (i, 0)),
        out_shape=jax.ShapeDtypeStruct((n, D_MODEL), F32),
        scratch_shapes=[pltpu.VMEM((D_MODEL, tn), F32), pltpu.VMEM((te // 2, tn), BF16),
                        pltpu.VMEM((te // 2, tn), BF16), pltpu.VMEM((D_MODEL, tn), BF16)],
        compiler_params=_cparams(("parallel", "arbitrary")),
    )(xn, x1, e1, nb, rk, e2, u, vt, gf)


def _tile(n, pref):
    return pref if n % pref == 0 else n


def _layer(x, pos, s_ret, s_c, s_n, s_m, s_conv, wts):
    b, t, _ = x.shape
    n = b * t
    x2d = x.reshape(n, D_MODEL)
    proj, ifc, ifr = _proj(x2d, wts["norm1_g"], wts["w_main"], wts["wif_col"], _tile(n, 1024))
    proj3 = proj.reshape(b, t, PROJ_MAIN)
    ifc3 = ifc.reshape(b, t, LANES)
    ifr3 = ifr.reshape(SUBLANES, b, t).transpose(1, 0, 2)
    y_r, ret_new = _retention(proj3, s_ret, pos, wts["ret_norm_g"])
    h_m, c_new, n_new, m_new = _mlstm(proj3, ifc3, ifr3, s_conv, s_c, s_n, s_m, wts["b_i"], wts["b_f"],
                                      wts["conv_w"], wts["conv_b"], wts["w_mq"], wts["w_mk"], wts["mlstm_norm_g"])
    conv_new = proj3[:, t - (CONV_W - 1):, 3 * D_MODEL:4 * D_MODEL]
    x1, xn2, xc2 = _merge(x2d, y_r.reshape(n, D_MODEL), h_m.reshape(n, D_MODEL), proj,
                     wts["w_br"], wts["w_bm"], wts["w_out"], wts["norm2_g"], _tile(n, 512))
    tn = _tile(n, 512)
    e1, nb, rk, e2 = _route(xn2, wts["w_pq_t"], wts["k1"], wts["k2"], tn)
    y = _peer(xc2, x1, e1, nb, rk, e2, wts["u"], wts["vt"], wts["norm_f_g"], tn)
    return (y.reshape(b, t, D_MODEL), ret_new, c_new, n_new.reshape(b, H_M, DH_M), m_new[:, :, 0, 0], conv_new)


def kernel(x_prompt, x_sample, state_ret, state_mlstm_C, state_mlstm_n, state_mlstm_m, state_conv, norm1_g, w_in, b_i, b_f, conv_w, conv_b, w_mq, w_mk, ret_norm_g, mlstm_norm_g, w_br, w_bm, w_out, norm2_g, w_pq, sub_keys1, sub_keys2, expert_u, expert_v, norm_f_g):
    assert w_in.shape[0] == 1, "single-layer step"
    bp, tp, _ = x_prompt.shape
    ts = x_sample.shape[1]
    qk, vr, wm = H_R * DK_R, H_R * DV_R, H_M * DH_M
    w = w_in[0]
    o_if = 2 * qk + 2 * vr + 3 * wm
    w_main = jnp.concatenate([w[:, :o_if], w[:, o_if + 2 * H_M:]], axis=1).astype(BF16)
    w_if = w[:, o_if:o_if + 2 * H_M]
    wts = dict(
        norm1_g=norm1_g[0].reshape(1, D_MODEL),
        w_main=w_main,
        wif_col=jnp.zeros((D_MODEL, LANES), F32).at[:, :2 * H_M].set(w_if),
        b_i=b_i[0], b_f=b_f[0], conv_w=conv_w[0], conv_b=conv_b[0], w_mq=w_mq[0], w_mk=w_mk[0],
        ret_norm_g=ret_norm_g[0], mlstm_norm_g=mlstm_norm_g[0],
        w_br=w_br[0].astype(BF16), w_bm=w_bm[0].astype(BF16), w_out=w_out[0].astype(BF16),
        norm2_g=norm2_g[0].reshape(1, D_MODEL),
        w_pq_t=w_pq[0].T.astype(BF16),
        k1=sub_keys1[0].astype(BF16), k2=sub_keys2[0].astype(BF16),
        u=expert_u[0].astype(BF16), vt=expert_v[0].T.astype(BF16),
        norm_f_g=norm_f_g.reshape(1, D_MODEL),
    )
    pos_p = jnp.arange(tp, dtype=F32)
    pos_s = PAST_LEN + jnp.arange(ts, dtype=F32)
    zp = lambda *s: jnp.zeros((bp,) + s, F32)
    yp, r0, c0, n0, m0, v0 = _layer(x_prompt, pos_p, zp(H_R, DK_R, DV_R), zp(H_M, DH_M, DH_M), zp(H_M, DH_M),
                                    zp(H_M), zp(CONV_W - 1, wm), wts)
    ys, r1, c1, n1, m1, v1 = _layer(x_sample, pos_s, state_ret[0], state_mlstm_C[0], state_mlstm_n[0],
                                    state_mlstm_m[0], state_conv[0], wts)
    return (yp, ys, r0[None], c0[None], n0[None], m0[None], v0[None],
            r1[None], c1[None], n1[None], m1[None], v1[None])
```
